```python
import jax, jax.numpy as jnp
from jax import lax
import numpy as np

D_MODEL = 1024
BATCH = 8
SEQ = 4096
DEPTH = 2

HEAD_DIM = 64
D_MIX = D_MODEL
NSA_HEADS = (D_MIX // 2) // HEAD_DIM
NSA_KV_HEADS = 2
NSA_GROUP = NSA_HEADS // NSA_KV_HEADS
CMP_LEN = 32
CMP_STRIDE = 16
CMP_HIDDEN = 256
SLC_BLOCK = 64
SLC_TOPK = 16
WINDOW = 512
WIN_BLOCK = 128
SEL_QBLOCK = 64
N_BRANCH = 3
CONV_CH = D_MIX // 4
CONV_K = 31
RET_HEADS = (D_MIX // 4) // HEAD_DIM
RET_CHUNK = 128
ROPE_BASE = 10000.0
D_FF = ((8 * D_MODEL // 3 + 255) // 256) * 256
N_EXPERTS = 8
TOP_K = 2
MOE_BLOCK = 256
NSA_Q_W = NSA_HEADS * HEAD_DIM
NSA_KV_W = NSA_KV_HEADS * HEAD_DIM
NSA_GATE_W = NSA_HEADS * N_BRANCH
RET_W = RET_HEADS * HEAD_DIM
IN_SPLITS = (NSA_Q_W, NSA_KV_W, NSA_KV_W, NSA_KV_W, NSA_KV_W, NSA_KV_W, NSA_KV_W, NSA_GATE_W, 2 * CONV_CH, RET_W, RET_W, RET_W, RET_W)
D_IN = NSA_Q_W + 6 * NSA_KV_W + NSA_GATE_W + 2 * CONV_CH + 4 * RET_W
NEG_INF = -1e30
FORCE_SCORE = 1e9
EPS = 1e-6

kernel_name = 'hybrid_nsa_conformer_retention_moe'


def rmsnorm(x, g):
    xf = x.astype(jnp.float32)
    xf = xf * lax.rsqrt(jnp.mean(xf * xf, axis=-1, keepdims=True) + EPS)
    return (xf * g.astype(jnp.float32)).astype(x.dtype)


def masked_softmax(s, mask):
    s = jnp.where(mask, s.astype(jnp.float32), NEG_INF)
    p = jax.nn.softmax(s, axis=-1)
    return jnp.where(mask, p, 0.0)


def compress_blocks(kv, pe, w1, w2):
    B, T, G, hd = kv.shape
    r = CMP_LEN // CMP_STRIDE
    ns = T // CMP_STRIDE
    sub = kv.reshape(B, ns, CMP_STRIDE, G, hd)
    blocks = jnp.concatenate([sub[:, j:ns - r + 1 + j] for j in range(r)], axis=2)
    blocks = blocks + pe[None, None, :, None, :]
    nc = blocks.shape[1]
    flat = blocks.transpose(0, 1, 3, 2, 4).reshape(B, nc, G, CMP_LEN * hd)
    return jax.nn.silu(flat @ w1) @ w2


def nsa_mixer(q, k_c, v_c, k_s, v_s, k_w, v_w, gate_logits, cmp_pe, cmp_w1, cmp_w2):
    B, T, _ = q.shape
    G, HPG, hd = NSA_KV_HEADS, NSA_GROUP, HEAD_DIM
    scale = hd ** -0.5
    q5 = q.reshape(B, T, G, HPG, hd)
    kv4 = lambda a: a.reshape(B, T, G, hd)
    k_c, v_c, k_s, v_s, k_w, v_w = map(kv4, (k_c, v_c, k_s, v_s, k_w, v_w))
    t_pos = jnp.arange(T)

    kc = compress_blocks(k_c, cmp_pe[0], cmp_w1[0], cmp_w2[0])
    vc = compress_blocks(v_c, cmp_pe[1], cmp_w1[1], cmp_w2[1])
    nc = kc.shape[1]
    cmp_start = jnp.arange(nc) * CMP_STRIDE
    cmp_mask = (cmp_start + CMP_LEN - 1)[None, :] <= t_pos[:, None]
    s_cmp = jnp.einsum('btghd,bcgd->bghtc', q5, kc) * scale
    p_cmp = masked_softmax(s_cmp, cmp_mask[None, None, None])
    o_cmp = jnp.einsum('bghtc,bcgd->btghd', p_cmp, vc.astype(jnp.float32))

    ns = T // SLC_BLOCK
    slc_start = jnp.arange(ns) * SLC_BLOCK
    overlap = ((cmp_start[:, None] < slc_start[None, :] + SLC_BLOCK) &
               (cmp_start[:, None] + CMP_LEN > slc_start[None, :])).astype(jnp.float32)
    imp = jnp.einsum('bgtc,cs->bgts', p_cmp.sum(axis=2), overlap)
    cur = t_pos // SLC_BLOCK
    j = jnp.arange(ns)
    valid = j[None, :] <= cur[:, None]
    forced = (j[None, :] == 0) | (j[None, :] == cur[:, None]) | (j[None, :] == cur[:, None] - 1)
    imp = jnp.where(valid, jnp.where(forced, FORCE_SCORE, imp), NEG_INF)
    k_top = min(SLC_TOPK, ns)
    _, idx = lax.top_k(imp, k_top)

    ks_blocks = k_s.reshape(B, ns, SLC_BLOCK, G, hd).transpose(0, 3, 1, 2, 4)
    vs_blocks = v_s.reshape(B, ns, SLC_BLOCK, G, hd).transpose(0, 3, 1, 2, 4)
    nq = T // SEL_QBLOCK
    q_ch = q5.reshape(B, nq, SEL_QBLOCK, G, HPG, hd).transpose(1, 0, 3, 2, 4, 5)
    idx_ch = idx.reshape(B, G, nq, SEL_QBLOCK, k_top).transpose(2, 0, 1, 3, 4)
    t_ch = t_pos.reshape(nq, SEL_QBLOCK)
    gather = jax.vmap(jax.vmap(lambda blk, ix: blk[ix]))
    n_sel = k_top * SLC_BLOCK

    def sel_block(args):
        qb, ib, tb = args
        kg = gather(ks_blocks, ib).reshape(B, G, SEL_QBLOCK, n_sel, hd)
        vg = gather(vs_blocks, ib).reshape(B, G, SEL_QBLOCK, n_sel, hd)
        kpos = (ib[..., None] * SLC_BLOCK + jnp.arange(SLC_BLOCK)).reshape(B, G, SEL_QBLOCK, n_sel)
        mask = kpos <= tb[None, None, :, None]
        s = jnp.einsum('bgqhd,bgqnd->bgqhn', qb, kg) * scale
        p = masked_softmax(s, mask[:, :, :, None, :])
        return jnp.einsum('bgqhn,bgqnd->bgqhd', p, vg.astype(jnp.float32))

    o_sel = lax.map(sel_block, (q_ch, idx_ch, t_ch))
    o_sel = o_sel.transpose(1, 0, 3, 2, 4, 5).reshape(B, T, G, HPG, hd)

    nb = T // WIN_BLOCK
    nwb = WINDOW // WIN_BLOCK
    qb = q5.reshape(B, nb, WIN_BLOCK, G, HPG, hd)
    def band(a):
        a = a.reshape(B, nb, WIN_BLOCK, G, hd)
        a = jnp.pad(a, ((0, 0), (nwb, 0), (0, 0), (0, 0), (0, 0)))
        return jnp.concatenate([a[:, j:j + nb] for j in range(nwb + 1)], axis=2)
    kwin, vwin = band(k_w), band(v_w)
    qpos = jnp.arange(nb)[:, None] * WIN_BLOCK + jnp.arange(WIN_BLOCK)[None, :]
    kpos = (jnp.arange(nb)[:, None] - nwb) * WIN_BLOCK + jnp.arange((nwb + 1) * WIN_BLOCK)[None, :]
    dpos = qpos[:, :, None] - kpos[:, None, :]
    wmask = (dpos >= 0) & (dpos < WINDOW) & (kpos[:, None, :] >= 0)
    s_w = jnp.einsum('bnqghd,bnkgd->bnghqk', qb, kwin) * scale
    p_w = masked_softmax(s_w, wmask[None, :, None, None])
    o_win = jnp.einsum('bnghqk,bnkgd->bnqghd', p_w, vwin.astype(jnp.float32)).reshape(B, T, G, HPG, hd)

    g = jax.nn.sigmoid(gate_logits.astype(jnp.float32)).reshape(B, T, G, HPG, N_BRANCH)
    o = g[..., 0:1] * o_cmp + g[..., 1:2] * o_sel + g[..., 2:3] * o_win
    return o.reshape(B, T, NSA_Q_W).astype(q.dtype)


def conformer_conv(u, conv_w, conv_b, ln_g, ln_b, conv_pw):
    a, gt = jnp.split(u, 2, axis=-1)
    h = a * jax.nn.sigmoid(gt)
    h = lax.conv_general_dilated(h, conv_w[:, None, :].astype(h.dtype), window_strides=(1,),
                                 padding=[(CONV_K - 1, 0)], dimension_numbers=('NWC', 'WIO', 'NWC'),
                                 feature_group_count=CONV_CH) + conv_b
    hf = h.astype(jnp.float32)
    mu = jnp.mean(hf, axis=-1, keepdims=True)
    var = jnp.mean(jnp.square(hf - mu), axis=-1, keepdims=True)
    hf = (hf - mu) * lax.rsqrt(var + 1e-5) * ln_g + ln_b
    return jax.nn.silu(hf).astype(u.dtype) @ conv_pw


def rotary(x, pos):
    half = HEAD_DIM // 2
    inv = ROPE_BASE ** (-jnp.arange(half, dtype=jnp.float32) / half)
    ang = pos[:, None] * inv[None, :]
    cos, sin = jnp.cos(ang), jnp.sin(ang)
    x1, x2 = x[..., :half], x[..., half:]
    return jnp.concatenate([x1 * cos - x2 * sin, x1 * sin + x2 * cos], axis=-1)


def retention(q, k, v, gate, gn_g):
    B, T, _ = q.shape
    heads = lambda a: a.astype(jnp.float32).reshape(B, T, RET_HEADS, HEAD_DIM).transpose(0, 2, 1, 3)
    q, k, v = heads(q), heads(k), heads(v)
    pos = jnp.arange(T, dtype=jnp.float32)
    q = rotary(q, pos)
    k = rotary(k, pos) * HEAD_DIM ** -0.5
    log_gamma = jnp.log1p(-jnp.exp2(-5.0 - jnp.arange(RET_HEADS, dtype=jnp.float32)))
    i = jnp.arange(RET_CHUNK, dtype=jnp.float32)
    rel = i[:, None] - i[None, :]
    decay = jnp.where(rel >= 0, jnp.exp(jnp.maximum(rel, 0.0)[None] * log_gamma[:, None, None]), 0.0)
    xi = jnp.exp((i + 1.0)[None, :] * log_gamma[:, None])
    zeta = jnp.exp((RET_CHUNK - 1.0 - i)[None, :] * log_gamma[:, None])
    g_chunk = jnp.exp(RET_CHUNK * log_gamma)
    nc = T // RET_CHUNK
    chunks = lambda a: a.reshape(B, RET_HEADS, nc, RET_CHUNK, HEAD_DIM).transpose(2, 0, 1, 3, 4)
    qc, kc, vc = chunks(q), chunks(k), chunks(v)
    intra = jnp.einsum('nbhqd,nbhkd->nbhqk', qc, kc) * decay[None, None]
    intra = jnp.einsum('nbhqk,nbhkd->nbhqd', intra, vc)

    def step(S, inp):
        qi, ki, vi = inp
        cross = jnp.einsum('bhqd,bhde->bhqe', qi, S) * xi[None, :, :, None]
        S = g_chunk[None, :, None, None] * S + jnp.einsum('bhkd,bhke->bhde', ki * zeta[None, :, :, None], vi)
        return S, cross

    S0 = jnp.zeros((B, RET_HEADS, HEAD_DIM, HEAD_DIM), jnp.float32)
    _, cross = lax.scan(step, S0, (qc, kc, vc))
    o = (intra + cross).transpose(1, 0, 3, 2, 4).reshape(B, T, RET_HEADS, HEAD_DIM)
    mu = jnp.mean(o, axis=-1, keepdims=True)
    var = jnp.mean(jnp.square(o - mu), axis=-1, keepdims=True)
    o = ((o - mu) * lax.rsqrt(var + 1e-5)).reshape(B, T, RET_W) * gn_g
    return (jax.nn.silu(gate.astype(jnp.float32)) * o).astype(gate.dtype)


def token_mixer(h, w_in, cmp_pe, cmp_w1, cmp_w2, conv_w, conv_b, conv_ln_g, conv_ln_b, conv_pw, ret_gn, w_out):
    proj = h @ w_in
    offs = np.cumsum(IN_SPLITS)[:-1].tolist()
    (q, k_c, v_c, k_s, v_s, k_w, v_w, gl, conv_in, rq, rk, rv, rg) = jnp.split(proj, offs, axis=-1)
    o_a = nsa_mixer(q, k_c, v_c, k_s, v_s, k_w, v_w, gl, cmp_pe, cmp_w1, cmp_w2)
    o_b = conformer_conv(conv_in, conv_w, conv_b, conv_ln_g, conv_ln_b, conv_pw)
    o_c = retention(rq, rk, rv, rg, ret_gn)
    return jnp.concatenate([o_a, o_b, o_c], axis=-1) @ w_out


def swiglu(h, w13, w2):
    a, b = jnp.split(h @ w13, 2, axis=-1)
    return (jax.nn.silu(a) * b) @ w2


def moe_ffn(h, router, w13, w2):
    B, T, D = h.shape
    N = B * T
    xf = h.reshape(N, D)
    logits = (xf @ router).astype(jnp.float32)
    top_logit, top_e = lax.top_k(logits, TOP_K)
    gates = jax.nn.softmax(top_logit, axis=-1)
    e_flat = top_e.reshape(-1)
    tok_flat = jnp.repeat(jnp.arange(N, dtype=jnp.int32), TOP_K)
    g_flat = gates.reshape(-1)
    order = jnp.argsort(e_flat)
    e_sorted = e_flat[order]
    counts = jnp.bincount(e_flat, length=N_EXPERTS)
    padded = ((counts + MOE_BLOCK - 1) // MOE_BLOCK) * MOE_BLOCK
    starts = jnp.cumsum(counts) - counts
    pends = jnp.cumsum(padded)
    pstarts = pends - padded
    dest = pstarts[e_sorted] + (jnp.arange(N * TOP_K) - starts[e_sorted])
    P = ((N * TOP_K + MOE_BLOCK - 1) // MOE_BLOCK) * MOE_BLOCK + N_EXPERTS * MOE_BLOCK
    buf_tok = jnp.full((P,), N, jnp.int32).at[dest].set(tok_flat[order])
    buf_gate = jnp.zeros((P,), jnp.float32).at[dest].set(g_flat[order])
    nblk = P // MOE_BLOCK
    blk_e = jnp.minimum(jnp.searchsorted(pends, jnp.arange(nblk) * MOE_BLOCK, side='right'), N_EXPERTS - 1)
    x_pad = jnp.concatenate([xf, jnp.zeros((1, D), xf.dtype)], axis=0)
    xb = x_pad[buf_tok].reshape(nblk, MOE_BLOCK, D)
    yb = lax.map(lambda a: swiglu(a[0], w13[a[1]], w2[a[1]]), (xb, blk_e))
    y = jax.ops.segment_sum(yb.reshape(P, D) * buf_gate[:, None].astype(yb.dtype), buf_tok, num_segments=N + 1)
    return y[:N].reshape(B, T, D)


def setup_inputs(seed: int = 0) -> dict:
    key = jax.random.key(seed)
    ks = jax.random.split(key, 20)
    L = DEPTH
    ND = (DEPTH + 1) // 2
    NM = DEPTH // 2
    nrm = lambda k, shape, s: jax.random.normal(k, shape, jnp.float32) * s
    return {
        'x': nrm(ks[0], (BATCH, SEQ, D_MODEL), 1.0),
        'ln_attn': 1.0 + nrm(ks[1], (L, D_MODEL), 0.1),
        'w_in': nrm(ks[2], (L, D_MODEL, D_IN), D_MODEL ** -0.5),
        'cmp_pe': nrm(ks[3], (L, 2, CMP_LEN, HEAD_DIM), 0.1),
        'cmp_w1': nrm(ks[4], (L, 2, CMP_LEN * HEAD_DIM, CMP_HIDDEN), (CMP_LEN * HEAD_DIM) ** -0.5),
        'cmp_w2': nrm(ks[5], (L, 2, CMP_HIDDEN, HEAD_DIM), CMP_HIDDEN ** -0.5),
        'conv_w': nrm(ks[6], (L, CONV_K, CONV_CH), CONV_K ** -0.5),
        'conv_b': nrm(ks[7], (L, CONV_CH), 0.02),
        'conv_ln_g': 1.0 + nrm(ks[8], (L, CONV_CH), 0.1),
        'conv_ln_b': nrm(ks[9], (L, CONV_CH), 0.02),
        'conv_pw': nrm(ks[10], (L, CONV_CH, CONV_CH), CONV_CH ** -0.5),
        'ret_gn': 1.0 + nrm(ks[11], (L, RET_W), 0.1),
        'w_out': nrm(ks[12], (L, D_MIX, D_MODEL), D_MIX ** -0.5),
        'ln_ffn': 1.0 + nrm(ks[13], (L, D_MODEL), 0.1),
        'ffn_w13': nrm(ks[14], (ND, D_MODEL, 2 * D_FF), D_MODEL ** -0.5),
        'ffn_w2': nrm(ks[15], (ND, D_FF, D_MODEL), D_FF ** -0.5),
        'router': nrm(ks[16], (NM, D_MODEL, N_EXPERTS), D_MODEL ** -0.5),
        'moe_w13': nrm(ks[17], (NM, N_EXPERTS, D_MODEL, 2 * D_FF), D_MODEL ** -0.5),
        'moe_w2': nrm(ks[18], (NM, N_EXPERTS, D_FF, D_MODEL), D_FF ** -0.5),
        'ln_final': 1.0 + nrm(ks[19], (D_MODEL,), 0.1),
    }


def reference(x, ln_attn, w_in, cmp_pe, cmp_w1, cmp_w2, conv_w, conv_b, conv_ln_g, conv_ln_b, conv_pw,
              ret_gn, w_out, ln_ffn, ffn_w13, ffn_w2, router, moe_w13, moe_w2, ln_final):
    for l in range(DEPTH):
        h = rmsnorm(x, ln_attn[l])
        x = x + token_mixer(h, w_in[l], cmp_pe[l], cmp_w1[l], cmp_w2[l], conv_w[l], conv_b[l],
                            conv_ln_g[l], conv_ln_b[l], conv_pw[l], ret_gn[l], w_out[l])
        h = rmsnorm(x, ln_ffn[l])
        if l % 2 == 0:
            x = x + swiglu(h, ffn_w13[l // 2], ffn_w2[l // 2])
        else:
            x = x + moe_ffn(h, router[l // 2], moe_w13[l // 2], moe_w2[l // 2])
    return rmsnorm(x, ln_final)
```

```python
import functools

import numpy as np
import jax
import jax.numpy as jnp
from jax import lax
from jax.experimental import pallas as pl
from jax.experimental.pallas import tpu as pltpu

D_MODEL = 1024
HEAD_DIM = 64
NSA_HEADS = 8
NSA_KV_HEADS = 2
NSA_GROUP = NSA_HEADS // NSA_KV_HEADS
CMP_LEN = 32
CMP_STRIDE = 16
CMP_HIDDEN = 256
SLC_BLOCK = 64
SLC_TOPK = 16
WINDOW = 512
N_BRANCH = 3
CONV_CH = 256
CONV_K = 31
RET_HEADS = 4
RET_CHUNK = 128
ROPE_BASE = 10000.0
D_FF = 2816
N_EXPERTS = 8
TOP_K = 2
NEG_INF = -1e30
FORCE_SCORE = 1e9
EPS = 1e-6

NSA_Q_W = NSA_HEADS * HEAD_DIM
NSA_KV_W = NSA_KV_HEADS * HEAD_DIM
NSA_GATE_W = NSA_HEADS * N_BRANCH
RET_W = RET_HEADS * HEAD_DIM

LANES = 128
Q_TILE = 128
SEL_CHUNK = 256
ROW_TILE = 512
CONV_TILE = 512
CONV_HALO = 32
MOE_ROWS = 512
FFN_ROWS = 512
FF_TILE = 1408
VMEM_LIMIT = 56 * 1024 * 1024

BF16 = jnp.bfloat16
F32 = jnp.float32

HM_Q, HM_KC, HM_VC, HM_VS, HM_KW, HM_VW, HM_SLABS = 0, 8, 10, 12, 14, 16, 18


def _dot(a, b):
    return jnp.dot(a, b, preferred_element_type=F32)


def _dot_nt(a, b):
    return lax.dot_general(a, b, (((1,), (1,)), ((), ())), preferred_element_type=F32)


def _split3(x):
    hi = x.astype(BF16)
    r1 = x - hi.astype(F32)
    mid = r1.astype(BF16)
    lo = (r1 - mid.astype(F32)).astype(BF16)
    return hi, mid, lo


def _params(*sem):
    return pltpu.CompilerParams(dimension_semantics=sem, vmem_limit_bytes=VMEM_LIMIT)


W_HM = HM_SLABS * HEAD_DIM
W_KS = W_HM + NSA_KV_W
W_GATE = W_KS + 2 * LANES
W_CONV = W_GATE + 2 * CONV_CH
W_RET = W_CONV + 4 * RET_W


def _arrange_w_in(w_in):
    offs = np.cumsum([0, NSA_Q_W] + [NSA_KV_W] * 6 + [NSA_GATE_W, 2 * CONV_CH] + [RET_W] * 4)
    q, kc, vc, ks, vs, kw, vw, gl, conv, rq, rk, rv, rg = [w_in[:, offs[i]:offs[i + 1]] for i in range(13)]
    scale = HEAD_DIM ** -0.5
    gpad = jnp.zeros((D_MODEL, LANES - NSA_GATE_W // 2), w_in.dtype)
    gates = jnp.concatenate([gl[:, :NSA_GATE_W // 2], gpad, gl[:, NSA_GATE_W // 2:], gpad], axis=1)
    w = jnp.concatenate([q * scale, kc, vc, vs, kw, vw, ks, gates, conv, rq, rk * scale, rv, rg], axis=1)
    return w.astype(BF16)


def _norm_proj_body(x_ref, g_ref, w_ref, hm_ref, ksp_ref, gates_ref, conv_ref, ret_ref, *, seq):
    x = x_ref[...]
    ms = jnp.mean(x * x, axis=-1, keepdims=True)
    h = (x * lax.rsqrt(ms + EPS) * g_ref[...]).astype(BF16)
    res = _dot(h, w_ref[:, 0:W_HM])
    for j in range(HM_SLABS):
        hm_ref[j] = res[:, j * HEAD_DIM:(j + 1) * HEAD_DIM].astype(BF16)
    rows = x.shape[0]
    t0 = (pl.program_id(0) % (seq // rows)) * rows
    blk = (t0 + lax.broadcasted_iota(jnp.int32, (rows, HEAD_DIM), 0)) // SLC_BLOCK
    onehot = jnp.where(blk == lax.broadcasted_iota(jnp.int32, (rows, HEAD_DIM), 1), 1.0, 0.0)
    ks = _dot(h, w_ref[:, W_HM:W_KS])
    for g in range(NSA_KV_HEADS):
        ksp_ref[g] = jnp.concatenate([ks[:, g * HEAD_DIM:(g + 1) * HEAD_DIM], onehot], axis=1).astype(BF16)
    gates_ref[...] = _dot(h, w_ref[:, W_KS:W_GATE])
    conv_ref[...] = _dot(h, w_ref[:, W_GATE:W_CONV]).astype(BF16)
    ret_ref[...] = _dot(h, w_ref[:, W_CONV:W_RET]).astype(BF16)


def _norm_proj(x2, ln_g, w_arr, seq):
    n = x2.shape[0]
    tm = ROW_TILE
    return pl.pallas_call(
        functools.partial(_norm_proj_body, seq=seq),
        grid=(n // tm,),
        in_specs=[
            pl.BlockSpec((tm, D_MODEL), lambda i: (i, 0)),
            pl.BlockSpec((1, D_MODEL), lambda i: (0, 0)),
            pl.BlockSpec((D_MODEL, W_RET), lambda i: (0, 0)),
        ],
        out_specs=[
            pl.BlockSpec((HM_SLABS, tm, HEAD_DIM), lambda i: (0, i, 0)),
            pl.BlockSpec((NSA_KV_HEADS, tm, LANES), lambda i: (0, i, 0)),
            pl.BlockSpec((tm, 2 * LANES), lambda i: (i, 0)),
            pl.BlockSpec((tm, 2 * CONV_CH), lambda i: (i, 0)),
            pl.BlockSpec((tm, 4 * RET_W), lambda i: (i, 0)),
        ],
        out_shape=[
            jax.ShapeDtypeStruct((HM_SLABS, n, HEAD_DIM), BF16),
            jax.ShapeDtypeStruct((NSA_KV_HEADS, n, LANES), BF16),
            jax.ShapeDtypeStruct((n, 2 * LANES), F32),
            jax.ShapeDtypeStruct((n, 2 * CONV_CH), BF16),
            jax.ShapeDtypeStruct((n, 4 * RET_W), BF16),
        ],
        compiler_params=_params("parallel"),
        name="norm_proj",
    )(x2, ln_g.reshape(1, D_MODEL), w_arr)


def _compress_body(s_ref, pe_ref, w1_ref, w2_ref, o_ref):
    u = _dot(s_ref[...], w1_ref[...])
    pe = pe_ref[...]
    bias = (_dot(pe[:, :CMP_STRIDE * HEAD_DIM], w1_ref[:, :CMP_HIDDEN])
            + _dot(pe[:, CMP_STRIDE * HEAD_DIM:], w1_ref[:, CMP_HIDDEN:]))[0:1]
    nsub = u.shape[0]
    bot = u[:, CMP_HIDDEN:]
    bot_next = jnp.concatenate([bot[1:], bot[:1]], axis=0)
    hid = u[:, :CMP_HIDDEN] + bot_next + bias
    hid = hid * jax.nn.sigmoid(hid)
    del nsub
    o_ref[...] = _dot(hid.astype(BF16), w2_ref[...]).astype(BF16)


def _compress(sub, pe8, w1cat, w2):
    _, b, nsub, width = sub.shape
    return pl.pallas_call(
        _compress_body,
        grid=(4, b),
        in_specs=[
            pl.BlockSpec((None, None, nsub, width), lambda j, i: (j, i, 0, 0)),
            pl.BlockSpec((None, 8, 2 * width), lambda j, i: (j // 2, 0, 0)),
            pl.BlockSpec((None, width, 2 * CMP_HIDDEN), lambda j, i: (j // 2, 0, 0)),
            pl.BlockSpec((None, CMP_HIDDEN, HEAD_DIM), lambda j, i: (j // 2, 0, 0)),
        ],
        out_specs=pl.BlockSpec((None, None, nsub, HEAD_DIM), lambda j, i: (j, i, 0, 0)),
        out_shape=jax.ShapeDtypeStruct((4, b, nsub, HEAD_DIM), BF16),
        compiler_params=_params("parallel", "parallel"),
        name="compress",
    )(sub, pe8, w1cat, w2)


def _masked_softmax(s, mask):
    s = jnp.where(mask, s, NEG_INF)
    m = jnp.max(s, axis=-1, keepdims=True)
    p = jnp.where(mask, jnp.exp(s - m), 0.0)
    l = jnp.sum(p, axis=-1, keepdims=True)
    return p / jnp.where(l > 0.0, l, 1.0)


def _topk_block_bias(imp_t):
    ns, nq = imp_t.shape
    sub = 8
    nb = ns // sub
    blocks = [imp_t[v * sub:(v + 1) * sub] for v in range(nb)]
    cnts = [jnp.zeros((sub, nq), jnp.int32) for _ in range(nb)]
    sub_iota = lax.broadcasted_iota(jnp.int32, (sub, nq), 0)
    for i in range(ns):
        row = jnp.broadcast_to(imp_t[i:i + 1], (sub, nq))
        for v in range(nb):
            if v > i // sub:
                cnts[v] = cnts[v] + jnp.where(row >= blocks[v], 1, 0)
            elif v < i // sub:
                cnts[v] = cnts[v] + jnp.where(row > blocks[v], 1, 0)
            else:
                tie = (row == blocks[v]) & (sub_iota > i % sub)
                cnts[v] = cnts[v] + jnp.where(row > blocks[v], 1, 0) + jnp.where(tie, 1, 0)
    cnt = jnp.concatenate(cnts, axis=0)
    return jnp.where(cnt < SLC_TOPK, 0.0, NEG_INF)


def _nsa_body(q_ref, gl_ref, kc_ref, vc_ref, ksp_ref, vs_ref, kw_ref, vw_ref, ovt_ref, o_ref):
    i = pl.program_id(2)
    hpg = NSA_GROUP
    rows = hpg * Q_TILE
    q = q_ref[...].reshape(rows, HEAD_DIM)
    t_rows = i * Q_TILE + (lax.broadcasted_iota(jnp.int32, (rows, 1), 0) & (Q_TILE - 1))

    kc = kc_ref[...]
    ncb = kc.shape[0]
    s_c = _dot_nt(q, kc)
    c_end = lax.broadcasted_iota(jnp.int32, (1, ncb), 1) * CMP_STRIDE + (CMP_LEN - 1)
    p_c = _masked_softmax(s_c, c_end <= t_rows)
    o_cmp = _dot(p_c.astype(BF16), vc_ref[...])

    psum = p_c[0:Q_TILE]
    for h in range(1, hpg):
        psum = psum + p_c[h * Q_TILE:(h + 1) * Q_TILE]
    ovt = ovt_ref[...]
    imp_t = sum(_dot_nt(ovt, part) for part in _split3(psum))
    ns = imp_t.shape[0]
    sblk = lax.broadcasted_iota(jnp.int32, (ns, Q_TILE), 0)
    cur = (i * Q_TILE + lax.broadcasted_iota(jnp.int32, (ns, Q_TILE), 1)) // SLC_BLOCK
    forced = (sblk == 0) | (sblk == cur) | (sblk == cur - 1)
    imp_t = jnp.where(sblk <= cur, jnp.where(forced, FORCE_SCORE, imp_t), NEG_INF)
    bias_t = _topk_block_bias(imp_t)
    bias_sq = jnp.concatenate([bias_t, jnp.zeros((Q_TILE - ns, Q_TILE), F32)], axis=0).T
    bias = bias_sq[:, :ns].astype(BF16)
    qp = jnp.concatenate([q, jnp.concatenate([bias] * hpg, axis=0)], axis=1)

    def sel_step(c, carry):
        m, l, acc = carry
        k0 = pl.multiple_of(c * SEL_CHUNK, SEL_CHUNK)
        s = _dot_nt(qp, ksp_ref[pl.ds(k0, SEL_CHUNK), :])
        kpos = k0 + lax.broadcasted_iota(jnp.int32, (1, SEL_CHUNK), 1)
        s = jnp.where(kpos <= t_rows, s, NEG_INF)
        m_new = jnp.maximum(m, jnp.max(s, axis=-1, keepdims=True))
        alpha = jnp.exp(m - m_new)
        p = jnp.exp(s - m_new)
        l = alpha * l + jnp.sum(p, axis=-1, keepdims=True)
        acc = alpha * acc + _dot(p.astype(BF16), vs_ref[pl.ds(k0, SEL_CHUNK), :])
        return m_new, l, acc

    n_chunks = (i * Q_TILE + Q_TILE - 1) // SEL_CHUNK + 1
    init = (jnp.full((rows, 1), NEG_INF, F32), jnp.zeros((rows, 1), F32), jnp.zeros((rows, HEAD_DIM), F32))
    _, l_s, acc_s = lax.fori_loop(0, n_chunks, sel_step, init)
    o_sel = acc_s / l_s

    wkeys = WINDOW + Q_TILE
    w0 = pl.multiple_of(jnp.maximum(i * Q_TILE - WINDOW, 0), Q_TILE)
    s_w = _dot_nt(q, kw_ref[pl.ds(w0, wkeys), :])
    dpos = t_rows - (w0 + lax.broadcasted_iota(jnp.int32, (1, wkeys), 1))
    p_w = _masked_softmax(s_w, (dpos >= 0) & (dpos < WINDOW))
    o_win = _dot(p_w.astype(BF16), vw_ref[pl.ds(w0, wkeys), :])

    sig = jax.nn.sigmoid(gl_ref[...])
    outs = []
    for h in range(hpg):
        r = slice(h * Q_TILE, (h + 1) * Q_TILE)
        c = h * N_BRANCH
        outs.append(sig[:, c:c + 1] * o_cmp[r] + sig[:, c + 1:c + 2] * o_sel[r] + sig[:, c + 2:c + 3] * o_win[r])
    o_ref[...] = jnp.concatenate(outs, axis=1).astype(BF16)


def _nsa(hm, gates, kvc, ksp, ovt, batch, seq):
    n = batch * seq
    nq = seq // Q_TILE
    ncb = kvc.shape[2]
    g_ = NSA_KV_HEADS
    return pl.pallas_call(
        _nsa_body,
        grid=(batch, g_, nq),
        in_specs=[
            pl.BlockSpec((NSA_GROUP, Q_TILE, HEAD_DIM), lambda b, g, i: (g, b * nq + i, 0)),
            pl.BlockSpec((Q_TILE, LANES), lambda b, g, i: (b * nq + i, g)),
            pl.BlockSpec((None, None, ncb, HEAD_DIM), lambda b, g, i: (g, b, 0, 0)),
            pl.BlockSpec((None, None, ncb, HEAD_DIM), lambda b, g, i: (g_ + g, b, 0, 0)),
            pl.BlockSpec((None, seq, LANES), lambda b, g, i: (g, b, 0)),
            pl.BlockSpec((None, seq, HEAD_DIM), lambda b, g, i: (HM_VS + g, b, 0)),
            pl.BlockSpec((None, seq, HEAD_DIM), lambda b, g, i: (HM_KW + g, b, 0)),
            pl.BlockSpec((None, seq, HEAD_DIM), lambda b, g, i: (HM_VW + g, b, 0)),
            pl.BlockSpec(ovt.shape, lambda b, g, i: (0, 0)),
        ],
        out_specs=pl.BlockSpec((Q_TILE, NSA_GROUP * HEAD_DIM), lambda b, g, i: (b * nq + i, g)),
        out_shape=jax.ShapeDtypeStruct((n, NSA_Q_W), BF16),
        compiler_params=_params("parallel", "parallel", "arbitrary"),
        name="nsa",
    )(hm, gates, kvc, kvc, ksp, hm, hm, hm, ovt)


def _conv_body(u_ref, up_ref, w_ref, b_ref, lg_ref, lb_ref, pw_ref, o_ref, hbuf):
    i = pl.program_id(1)

    def glu(u):
        u = u.astype(F32)
        return u[:, :CONV_CH] * jax.nn.sigmoid(u[:, CONV_CH:])

    prev = glu(up_ref[...])
    hbuf[0:CONV_HALO] = jnp.where(i > 0, prev, 0.0)
    hbuf[CONV_HALO:] = glu(u_ref[...])
    w = w_ref[...]
    acc = jnp.zeros((CONV_TILE, CONV_CH), F32) + b_ref[...]
    base = CONV_HALO - (CONV_K - 1)
    for j in range(CONV_K):
        acc = acc + w[j:j + 1] * hbuf[base + j:base + j + CONV_TILE]
    mu = jnp.mean(acc, axis=-1, keepdims=True)
    var = jnp.mean(jnp.square(acc - mu), axis=-1, keepdims=True)
    hf = (acc - mu) * lax.rsqrt(var + 1e-5) * lg_ref[...] + lb_ref[...]
    hf = hf * jax.nn.sigmoid(hf)
    o_ref[...] = _dot(hf.astype(BF16), pw_ref[...]).astype(BF16)


def _conv(conv_in, w32, b, lg, lb, pw, batch, seq):
    n = batch * seq
    nt = seq // CONV_TILE
    halo_per_tile = CONV_TILE // CONV_HALO
    row = lambda v: v.reshape(1, CONV_CH)
    return pl.pallas_call(
        _conv_body,
        grid=(batch, nt),
        in_specs=[
            pl.BlockSpec((CONV_TILE, 2 * CONV_CH), lambda b_, i: (b_ * nt + i, 0)),
            pl.BlockSpec((CONV_HALO, 2 * CONV_CH),
                         lambda b_, i: (jnp.maximum((b_ * nt + i) * halo_per_tile - 1, 0), 0)),
            pl.BlockSpec((32, CONV_CH), lambda b_, i: (0, 0)),
            pl.BlockSpec((1, CONV_CH), lambda b_, i: (0, 0)),
            pl.BlockSpec((1, CONV_CH), lambda b_, i: (0, 0)),
            pl.BlockSpec((1, CONV_CH), lambda b_, i: (0, 0)),
            pl.BlockSpec((CONV_CH, CONV_CH), lambda b_, i: (0, 0)),
        ],
        out_specs=pl.BlockSpec((CONV_TILE, CONV_CH), lambda b_, i: (b_ * nt + i, 0)),
        out_shape=jax.ShapeDtypeStruct((n, CONV_CH), BF16),
        scratch_shapes=[pltpu.VMEM((CONV_HALO + CONV_TILE, CONV_CH), F32)],
        compiler_params=_params("parallel", "parallel"),
        name="conformer_conv",
    )(conv_in, conv_in, w32, row(b), row(lg), row(lb), pw)


def _ret_body(gch_ref, q_ref, k_ref, v_ref, g_ref, cos_ref, sin_ref, perm_ref, decay_ref, xi_ref, zeta_ref,
              gn_ref, o_ref, state):
    c = pl.program_id(1)

    @pl.when(c == 0)
    def _():
        state[...] = jnp.zeros_like(state)

    perm = perm_ref[...]
    cos = cos_ref[...]
    sin = sin_ref[...]

    def rope(x):
        return x.astype(F32) * cos + _dot(x, perm) * sin

    qr = rope(q_ref[...])
    kr = rope(k_ref[...])
    kz_t = (kr * zeta_ref[...]).T
    v = v_ref[...]
    xi = xi_ref[...]
    gate = g_ref[...].astype(F32)
    gn = gn_ref[...]
    outs = []
    for h in range(RET_HEADS):
        sl = slice(h * HEAD_DIM, (h + 1) * HEAD_DIM)
        qh = qr[:, sl].astype(BF16)
        kh = kr[:, sl].astype(BF16)
        vh = v[:, sl]
        a = _dot_nt(qh, kh) * decay_ref[h]
        s_h = state[h]
        o = _dot(a.astype(BF16), vh) + _dot(qh, s_h.astype(BF16)) * xi[:, sl]
        state[h] = gch_ref[h] * s_h + _dot(kz_t[sl].astype(BF16), vh)
        mu = jnp.mean(o, axis=-1, keepdims=True)
        var = jnp.mean(jnp.square(o - mu), axis=-1, keepdims=True)
        outs.append((o - mu) * lax.rsqrt(var + 1e-5))
    o = jnp.concatenate(outs, axis=1) * gn
    o_ref[...] = (gate * jax.nn.sigmoid(gate) * o).astype(BF16)


def _retention_tables(seq):
    half = HEAD_DIM // 2
    pos = jnp.arange(seq, dtype=F32)
    inv = ROPE_BASE ** (-jnp.arange(half, dtype=F32) / half)
    ang = pos[:, None] * inv[None, :]
    cos, sin = jnp.cos(ang), jnp.sin(ang)
    cos_t = jnp.tile(jnp.concatenate([cos, cos], axis=1), (1, RET_HEADS))
    sin_t = jnp.tile(jnp.concatenate([-sin, sin], axis=1), (1, RET_HEADS))
    log_gamma = jnp.log1p(-jnp.exp2(-5.0 - jnp.arange(RET_HEADS, dtype=F32)))
    i = jnp.arange(RET_CHUNK, dtype=F32)
    rel = i[:, None] - i[None, :]
    decay = jnp.where(rel >= 0, jnp.exp(jnp.maximum(rel, 0.0)[None] * log_gamma[:, None, None]), 0.0)
    xi = jnp.exp((i + 1.0)[None, :] * log_gamma[:, None])
    zeta = jnp.exp((RET_CHUNK - 1.0 - i)[None, :] * log_gamma[:, None])
    g_chunk = jnp.exp(RET_CHUNK * log_gamma)
    lanes = lambda a: jnp.repeat(a.T, HEAD_DIM, axis=1)
    j = np.arange(RET_W)
    partner = (j // HEAD_DIM) * HEAD_DIM + (j % HEAD_DIM + half) % HEAD_DIM
    perm = np.zeros((RET_W, RET_W), np.float32)
    perm[partner, j] = 1.0
    return cos_t, sin_t, jnp.asarray(perm, BF16), decay, lanes(xi), lanes(zeta), g_chunk


def _retention(ret, gn_g, tables, batch, seq):
    n = batch * seq
    nc = seq // RET_CHUNK
    cos_t, sin_t, perm, decay, xi, zeta, g_chunk = tables
    tile = lambda col: pl.BlockSpec((RET_CHUNK, RET_W), lambda b, c: (b * nc + c, col))
    const = lambda shape: pl.BlockSpec(shape, lambda b, c: (0,) * len(shape))
    return pl.pallas_call(
        _ret_body,
        grid=(batch, nc),
        in_specs=[
            pl.BlockSpec(memory_space=pltpu.SMEM),
            tile(0), tile(1), tile(2), tile(3),
            pl.BlockSpec((RET_CHUNK, RET_W), lambda b, c: (c, 0)),
            pl.BlockSpec((RET_CHUNK, RET_W), lambda b, c: (c, 0)),
            const((RET_W, RET_W)),
            const((RET_HEADS, RET_CHUNK, RET_CHUNK)),
            const((RET_CHUNK, RET_W)),
            const((RET_CHUNK, RET_W)),
            const((1, RET_W)),
        ],
        out_specs=pl.BlockSpec((RET_CHUNK, RET_W), lambda b, c: (b * nc + c, 0)),
        out_shape=jax.ShapeDtypeStruct((n, RET_W), BF16),
        scratch_shapes=[pltpu.VMEM((RET_HEADS, HEAD_DIM, HEAD_DIM), F32)],
        compiler_params=_params("parallel", "arbitrary"),
        name="retention",
    )(g_chunk, ret, ret, ret, ret, cos_t, sin_t, perm, decay, xi, zeta, gn_g.reshape(1, RET_W))


def _out_proj_body(*refs, with_router):
    if with_router:
        oa_ref, ob_ref, oc_ref, x_ref, w_ref, g_ref, rhi_ref, rlo_ref, xo_ref, h_ref, lg_ref = refs
    else:
        oa_ref, ob_ref, oc_ref, x_ref, w_ref, g_ref, xo_ref, h_ref = refs
    y = (_dot(oa_ref[...], w_ref[0:NSA_Q_W])
         + _dot(ob_ref[...], w_ref[NSA_Q_W:NSA_Q_W + CONV_CH])
         + _dot(oc_ref[...], w_ref[NSA_Q_W + CONV_CH:]))
    x = x_ref[...] + y
    xo_ref[...] = x
    ms = jnp.mean(x * x, axis=-1, keepdims=True)
    h = x * lax.rsqrt(ms + EPS) * g_ref[...]
    hi = h.astype(BF16)
    h_ref[...] = hi
    if with_router:
        lo = (h - hi.astype(F32)).astype(BF16)
        lg_ref[...] = _dot(hi, rhi_ref[...]) + _dot(lo, rhi_ref[...]) + _dot(hi, rlo_ref[...])


def _out_proj(oa, ob, oc, x2, w_out, ln_g, router=None):
    n = x2.shape[0]
    tm = ROW_TILE
    with_router = router is not None
    row = lambda w: pl.BlockSpec((tm, w), lambda i: (i, 0))
    const = lambda shape: pl.BlockSpec(shape, lambda i: (0, 0))
    in_specs = [row(NSA_Q_W), row(CONV_CH), row(RET_W), row(D_MODEL), const((D_MODEL, D_MODEL)), const((1, D_MODEL))]
    args = [oa, ob, oc, x2, w_out.astype(BF16), ln_g.reshape(1, D_MODEL)]
    out_specs = [row(D_MODEL), row(D_MODEL)]
    out_shape = [jax.ShapeDtypeStruct((n, D_MODEL), F32), jax.ShapeDtypeStruct((n, D_MODEL), BF16)]
    if with_router:
        rpad = jnp.pad(router, ((0, 0), (0, LANES - N_EXPERTS)))
        rhi = rpad.astype(BF16)
        rlo = (rpad - rhi.astype(F32)).astype(BF16)
        in_specs += [const((D_MODEL, LANES)), const((D_MODEL, LANES))]
        args += [rhi, rlo]
        out_specs.append(row(LANES))
        out_shape.append(jax.ShapeDtypeStruct((n, LANES), F32))
    return pl.pallas_call(
        functools.partial(_out_proj_body, with_router=with_router),
        grid=(n // tm,),
        in_specs=in_specs,
        out_specs=out_specs,
        out_shape=out_shape,
        compiler_params=_params("parallel"),
        name="out_proj_router" if with_router else "out_proj",
    )(*args)


def _ffn_body(*refs, with_residual):
    if with_residual:
        e_ref, x_ref, wa_ref, wb_ref, w2_ref, r_ref, o_ref, acc = refs
    else:
        e_ref, x_ref, wa_ref, wb_ref, w2_ref, o_ref, acc = refs
    del e_ref
    f = pl.program_id(1)

    @pl.when(f == 0)
    def _():
        acc[...] = jnp.zeros_like(acc)

    x = x_ref[...]
    a = _dot(x, wa_ref[...])
    b = _dot(x, wb_ref[...])
    hmid = (a * jax.nn.sigmoid(a) * b).astype(BF16)
    acc[...] += _dot(hmid, w2_ref[...])

    @pl.when(f == pl.num_programs(1) - 1)
    def _():
        o_ref[...] = acc[...] + r_ref[...] if with_residual else acc[...]


def _ffn(xs, blk_e, w13, w2, rows, residual=None):
    p = xs.shape[0]
    nf = D_FF // FF_TILE
    with_residual = residual is not None
    in_specs = [
        pl.BlockSpec((rows, D_MODEL), lambda i, f, e: (i, 0)),
        pl.BlockSpec((None, D_MODEL, FF_TILE), lambda i, f, e: (e[i], 0, f)),
        pl.BlockSpec((None, D_MODEL, FF_TILE), lambda i, f, e: (e[i], 0, nf + f)),
        pl.BlockSpec((None, FF_TILE, D_MODEL), lambda i, f, e: (e[i], f, 0)),
    ]
    args = [xs, w13, w13, w2]
    if with_residual:
        in_specs.append(pl.BlockSpec((rows, D_MODEL), lambda i, f, e: (i, 0)))
        args.append(residual)
    return pl.pallas_call(
        functools.partial(_ffn_body, with_residual=with_residual),
        grid_spec=pltpu.PrefetchScalarGridSpec(
            num_scalar_prefetch=1,
            grid=(p // rows, nf),
            in_specs=in_specs,
            out_specs=pl.BlockSpec((rows, D_MODEL), lambda i, f, e: (i, 0)),
            scratch_shapes=[pltpu.VMEM((rows, D_MODEL), F32)],
        ),
        out_shape=jax.ShapeDtypeStruct((p, D_MODEL), F32),
        compiler_params=_params("parallel", "arbitrary"),
        name="ffn_residual" if with_residual else "ffn_grouped",
    )(blk_e, *args)


def _final_body(x_ref, y0_ref, y1_ref, gt_ref, g_ref, o_ref):
    gt = gt_ref[...]
    x = x_ref[...] + gt[:, 0:1] * y0_ref[...] + gt[:, 1:2] * y1_ref[...]
    ms = jnp.mean(x * x, axis=-1, keepdims=True)
    o_ref[...] = x * lax.rsqrt(ms + EPS) * g_ref[...]


def _final(x2, y0, y1, gates_pad, ln_g):
    n = x2.shape[0]
    tm = ROW_TILE
    row = lambda w: pl.BlockSpec((tm, w), lambda i: (i, 0))
    return pl.pallas_call(
        _final_body,
        grid=(n // tm,),
        in_specs=[row(D_MODEL), row(D_MODEL), row(D_MODEL), row(LANES), pl.BlockSpec((1, D_MODEL), lambda i: (0, 0))],
        out_specs=row(D_MODEL),
        out_shape=jax.ShapeDtypeStruct((n, D_MODEL), F32),
        compiler_params=_params("parallel"),
        name="moe_combine_final_norm",
    )(x2, y0, y1, gates_pad, ln_g.reshape(1, D_MODEL))


def _overlap_t(seq):
    nc = seq // CMP_STRIDE
    ns = seq // SLC_BLOCK
    cs = np.arange(nc)[None, :] * CMP_STRIDE
    ss = np.arange(ns)[:, None] * SLC_BLOCK
    return jnp.asarray(((cs < ss + SLC_BLOCK) & (cs + CMP_LEN > ss)).astype(np.float32), BF16)


def _route(logits, rows):
    n = logits.shape[0]
    top_logit, top_e = lax.top_k(logits, TOP_K)
    gates = jax.nn.softmax(top_logit, axis=-1)
    e_flat = top_e.reshape(-1)
    order = jnp.argsort(e_flat)
    e_sorted = e_flat[order]
    counts = jnp.bincount(e_flat, length=N_EXPERTS)
    padded = ((counts + rows - 1) // rows) * rows
    starts = jnp.cumsum(counts) - counts
    pends = jnp.cumsum(padded)
    pstarts = pends - padded
    dest = (pstarts[e_sorted] + (jnp.arange(n * TOP_K) - starts[e_sorted])).astype(jnp.int32)
    p = n * TOP_K + N_EXPERTS * rows
    buf_tok = jnp.full((p,), n, jnp.int32).at[dest].set((order // TOP_K).astype(jnp.int32))
    pos = jnp.zeros((n * TOP_K,), jnp.int32).at[order].set(dest).reshape(n, TOP_K)
    nblk = p // rows
    blk_e = jnp.minimum(jnp.searchsorted(pends, jnp.arange(nblk) * rows, side='right'), N_EXPERTS - 1)
    return gates, buf_tok, pos, blk_e.astype(jnp.int32)


def _token_mixer(x2, l, batch, seq, tables, ovt, ln_attn, w_in, cmp_pe, cmp_w1, cmp_w2, conv_w, conv_b, conv_ln_g,
                 conv_ln_b, conv_pw, ret_gn):
    hm, ksp, gates, conv_in, ret = _norm_proj(x2, ln_attn[l], _arrange_w_in(w_in[l]), seq)
    nsub = seq // CMP_STRIDE
    half = CMP_STRIDE * HEAD_DIM
    sub = hm[HM_KC:HM_VS].reshape(4, batch, nsub, half)
    w1 = cmp_w1[l]
    w1cat = jnp.concatenate([w1[:, :half], w1[:, half:]], axis=2).astype(BF16)
    pe8 = jnp.broadcast_to(cmp_pe[l].reshape(2, 1, CMP_LEN * HEAD_DIM), (2, 8, CMP_LEN * HEAD_DIM)).astype(BF16)
    kvc = _compress(sub, pe8, w1cat, cmp_w2[l].astype(BF16))
    o_a = _nsa(hm, gates, kvc, ksp, ovt, batch, seq)
    w32 = jnp.pad(conv_w[l], ((0, 32 - CONV_K), (0, 0)))
    o_b = _conv(conv_in, w32, conv_b[l], conv_ln_g[l], conv_ln_b[l], conv_pw[l].astype(BF16), batch, seq)
    o_c = _retention(ret, ret_gn[l], tables, batch, seq)
    return o_a, o_b, o_c


def kernel(x, ln_attn, w_in, cmp_pe, cmp_w1, cmp_w2, conv_w, conv_b, conv_ln_g, conv_ln_b, conv_pw, ret_gn, w_out,
           ln_ffn, ffn_w13, ffn_w2, router, moe_w13, moe_w2, ln_final):
    batch, seq, d = x.shape
    n = batch * seq
    depth = ln_attn.shape[0]
    x2 = x.reshape(n, d)
    tables = _retention_tables(seq)
    ovt = _overlap_t(seq)
    out = None
    for l in range(depth):
        o_a, o_b, o_c = _token_mixer(x2, l, batch, seq, tables, ovt, ln_attn, w_in, cmp_pe, cmp_w1, cmp_w2, conv_w,
                                     conv_b, conv_ln_g, conv_ln_b, conv_pw, ret_gn)
        if l % 2 == 0:
            x2, h2 = _out_proj(o_a, o_b, o_c, x2, w_out[l], ln_ffn[l])
            zeros_e = jnp.zeros((n // FFN_ROWS,), jnp.int32)
            x2 = _ffn(h2, zeros_e, ffn_w13[l // 2][None].astype(BF16), ffn_w2[l // 2][None].astype(BF16), FFN_ROWS,
                      residual=x2)
        else:
            x2, h2, logits = _out_proj(o_a, o_b, o_c, x2, w_out[l], ln_ffn[l], router=router[l // 2])
            gates, buf_tok, pos, blk_e = _route(logits[:, :N_EXPERTS], MOE_ROWS)
            h2_pad = jnp.concatenate([h2, jnp.zeros((1, d), h2.dtype)], axis=0)
            xs = h2_pad[buf_tok]
            yb = _ffn(xs, blk_e, moe_w13[l // 2].astype(BF16), moe_w2[l // 2].astype(BF16), MOE_ROWS)
            gates_pad = jnp.pad(gates, ((0, 0), (0, LANES - TOP_K)))
            if l == depth - 1:
                out = _final(x2, yb[pos[:, 0]], yb[pos[:, 1]], gates_pad, ln_final)
            else:
                x2 = x2 + gates[:, 0:1] * yb[pos[:, 0]] + gates[:, 1:2] * yb[pos[:, 1]]
    if out is None:
        out = _final(x2, jnp.zeros_like(x2), jnp.zeros_like(x2), jnp.zeros((n, LANES), F32), ln_final)
    return out.reshape(batch, seq, d)
```

```python
import functools

import numpy as np
import jax
import jax.numpy as jnp
from jax import lax
from jax.experimental import pallas as pl
from jax.experimental.pallas import tpu as pltpu

D_MODEL = 1024
HEAD_DIM = 64
NSA_HEADS = 8
NSA_KV_HEADS = 2
NSA_GROUP = NSA_HEADS // NSA_KV_HEADS
CMP_LEN = 32
CMP_STRIDE = 16
CMP_HIDDEN = 256
SLC_BLOCK = 64
SLC_TOPK = 16
WINDOW = 512
N_BRANCH = 3
CONV_CH = 256
CONV_K = 31
RET_HEADS = 4
RET_CHUNK = 128
ROPE_BASE = 10000.0
D_FF = 2816
N_EXPERTS = 8
TOP_K = 2
NEG_INF = -1e30
FORCE_SCORE = 1e9
EPS = 1e-6

NSA_Q_W = NSA_HEADS * HEAD_DIM
NSA_KV_W = NSA_KV_HEADS * HEAD_DIM
NSA_GATE_W = NSA_HEADS * N_BRANCH
RET_W = RET_HEADS * HEAD_DIM

LANES = 128
Q_TILE = 256
SEL_CHUNK = 512
ROW_TILE = 512
RET_BATCH = 4
CONV_TILE = 512
CONV_HALO = 32
MOE_ROWS = 512
FFN_ROWS = 512
FF_TILE = 1408
VMEM_LIMIT = 56 * 1024 * 1024

BF16 = jnp.bfloat16
F32 = jnp.float32

HM_Q, HM_KC, HM_VC, HM_KW, HM_SLABS = 0, 8, 10, 12, 14
WD_KS, WD_VS, WD_VW, WD_SLABS = 0, 2, 4, 6
LOG2E = 1.4426950408889634


def _dot(a, b):
    return jnp.dot(a, b, preferred_element_type=F32)


def _dot_nt(a, b):
    return lax.dot_general(a, b, (((1,), (1,)), ((), ())), preferred_element_type=F32)


def _split3(x):
    hi = x.astype(BF16)
    r1 = x - hi.astype(F32)
    mid = r1.astype(BF16)
    lo = (r1 - mid.astype(F32)).astype(BF16)
    return hi, mid, lo


def _params(*sem):
    return pltpu.CompilerParams(dimension_semantics=sem, vmem_limit_bytes=VMEM_LIMIT)


W_HM = HM_SLABS * HEAD_DIM
W_WD = W_HM + 3 * NSA_KV_W
W_GATE = W_WD + 2 * LANES
W_CONV = W_GATE + 2 * CONV_CH
W_RET = W_CONV + 4 * RET_W


def _arrange_w_in(w_in):
    offs = np.cumsum([0, NSA_Q_W] + [NSA_KV_W] * 6 + [NSA_GATE_W, 2 * CONV_CH] + [RET_W] * 4)
    q, kc, vc, ks, vs, kw, vw, gl, conv, rq, rk, rv, rg = [w_in[:, offs[i]:offs[i + 1]] for i in range(13)]
    scale = HEAD_DIM ** -0.5
    gpad = jnp.zeros((D_MODEL, LANES - NSA_GATE_W // 2), w_in.dtype)
    gates = jnp.concatenate([gl[:, :NSA_GATE_W // 2], gpad, gl[:, NSA_GATE_W // 2:], gpad], axis=1)
    w = jnp.concatenate([q * (scale * LOG2E), kc, vc, kw, ks, vs, vw, gates, conv, rq, rk * scale, rv, rg], axis=1)
    return w.astype(BF16)


def _norm_proj_body(x_ref, g_ref, w_ref, hm_ref, wd_ref, gates_ref, conv_ref, ret_ref, *, seq):
    x = x_ref[...]
    ms = jnp.mean(x * x, axis=-1, keepdims=True)
    h = (x * lax.rsqrt(ms + EPS) * g_ref[...]).astype(BF16)
    res = _dot(h, w_ref[:, 0:W_HM])
    for j in range(HM_SLABS):
        hm_ref[j] = res[:, j * HEAD_DIM:(j + 1) * HEAD_DIM].astype(BF16)
    rows = x.shape[0]
    t0 = (pl.program_id(0) % (seq // rows)) * rows
    lane = lax.broadcasted_iota(jnp.int32, (rows, HEAD_DIM), 1)
    blk = (t0 + lax.broadcasted_iota(jnp.int32, (rows, HEAD_DIM), 0)) // SLC_BLOCK
    onehot = jnp.where(blk == lane, 1.0, 0.0)
    ones_col = jnp.where(lane == 0, 1.0, 0.0)
    wide = _dot(h, w_ref[:, W_HM:W_WD])
    for j, ext in ((WD_KS, onehot), (WD_VS, ones_col), (WD_VW, ones_col)):
        for g in range(NSA_KV_HEADS):
            c = (j + g) * HEAD_DIM
            wd_ref[j + g] = jnp.concatenate([wide[:, c:c + HEAD_DIM], ext], axis=1).astype(BF16)
    gates_ref[...] = _dot(h, w_ref[:, W_WD:W_GATE])
    conv_ref[...] = _dot(h, w_ref[:, W_GATE:W_CONV]).astype(BF16)
    ret_ref[...] = _dot(h, w_ref[:, W_CONV:W_RET]).astype(BF16)


def _norm_proj(x2, ln_g, w_arr, seq):
    n = x2.shape[0]
    tm = ROW_TILE
    return pl.pallas_call(
        functools.partial(_norm_proj_body, seq=seq),
        grid=(n // tm,),
        in_specs=[
            pl.BlockSpec((tm, D_MODEL), lambda i: (i, 0)),
            pl.BlockSpec((1, D_MODEL), lambda i: (0, 0)),
            pl.BlockSpec((D_MODEL, W_RET), lambda i: (0, 0)),
        ],
        out_specs=[
            pl.BlockSpec((HM_SLABS, tm, HEAD_DIM), lambda i: (0, i, 0)),
            pl.BlockSpec((WD_SLABS, tm, LANES), lambda i: (0, i, 0)),
            pl.BlockSpec((tm, 2 * LANES), lambda i: (i, 0)),
            pl.BlockSpec((tm, 2 * CONV_CH), lambda i: (i, 0)),
            pl.BlockSpec((tm, 4 * RET_W), lambda i: (i, 0)),
        ],
        out_shape=[
            jax.ShapeDtypeStruct((HM_SLABS, n, HEAD_DIM), BF16),
            jax.ShapeDtypeStruct((WD_SLABS, n, LANES), BF16),
            jax.ShapeDtypeStruct((n, 2 * LANES), F32),
            jax.ShapeDtypeStruct((n, 2 * CONV_CH), BF16),
            jax.ShapeDtypeStruct((n, 4 * RET_W), BF16),
        ],
        compiler_params=_params("parallel"),
        name="norm_proj",
    )(x2, ln_g.reshape(1, D_MODEL), w_arr)


def _compress_body(s_ref, pe_ref, w1_ref, w2_ref, o_ref):
    u = _dot(s_ref[...], w1_ref[...])
    pe = pe_ref[...]
    bias = (_dot(pe[:, :CMP_STRIDE * HEAD_DIM], w1_ref[:, :CMP_HIDDEN])
            + _dot(pe[:, CMP_STRIDE * HEAD_DIM:], w1_ref[:, CMP_HIDDEN:]))[0:1]
    nsub = u.shape[0]
    bot = u[:, CMP_HIDDEN:]
    bot_next = jnp.concatenate([bot[1:], bot[:1]], axis=0)
    hid = u[:, :CMP_HIDDEN] + bot_next + bias
    hid = hid * jax.nn.sigmoid(hid)
    del nsub
    o_ref[...] = _dot(hid.astype(BF16), w2_ref[...]).astype(BF16)


def _compress(sub, pe8, w1cat, w2):
    _, b, nsub, width = sub.shape
    return pl.pallas_call(
        _compress_body,
        grid=(4, b),
        in_specs=[
            pl.BlockSpec((None, None, nsub, width), lambda j, i: (j, i, 0, 0)),
            pl.BlockSpec((None, 8, 2 * width), lambda j, i: (j // 2, 0, 0)),
            pl.BlockSpec((None, width, 2 * CMP_HIDDEN), lambda j, i: (j // 2, 0, 0)),
            pl.BlockSpec((None, CMP_HIDDEN, HEAD_DIM), lambda j, i: (j // 2, 0, 0)),
        ],
        out_specs=pl.BlockSpec((None, None, nsub, HEAD_DIM), lambda j, i: (j, i, 0, 0)),
        out_shape=jax.ShapeDtypeStruct((4, b, nsub, HEAD_DIM), BF16),
        compiler_params=_params("parallel", "parallel"),
        name="compress",
    )(sub, pe8, w1cat, w2)


def _topk_block_bias(imp_t):
    ns, nq = imp_t.shape
    sub = 8
    nb = ns // sub
    blocks = [imp_t[v * sub:(v + 1) * sub] for v in range(nb)]
    cnts = [jnp.zeros((sub, nq), jnp.int32) for _ in range(nb)]
    sub_iota = lax.broadcasted_iota(jnp.int32, (sub, nq), 0)
    for i in range(ns):
        row = jnp.broadcast_to(imp_t[i:i + 1], (sub, nq))
        for v in range(nb):
            if v > i // sub:
                cnts[v] = cnts[v] + jnp.where(row >= blocks[v], 1, 0)
            elif v < i // sub:
                cnts[v] = cnts[v] + jnp.where(row > blocks[v], 1, 0)
            else:
                tie = (row == blocks[v]) & (sub_iota > i % sub)
                cnts[v] = cnts[v] + jnp.where(row > blocks[v], 1, 0) + jnp.where(tie, 1, 0)
    cnt = jnp.concatenate(cnts, axis=0)
    return jnp.where(cnt < SLC_TOPK, 0.0, NEG_INF)


def _flash_update(qs, k, v1, bias, state):
    nh = len(state)
    qn = qs.shape[0] // nh
    s_all = _dot_nt(qs, k)
    ps, ms, alphas = [], [], []
    for h in range(nh):
        s = s_all[h * qn:(h + 1) * qn]
        if bias is not None:
            s = s + bias
        m = state[h][0]
        m_new = jnp.maximum(m, jnp.max(s, axis=-1, keepdims=True))
        ps.append(jnp.exp2(s - m_new).astype(BF16))
        alphas.append(jnp.exp2(m - m_new))
        ms.append(m_new)
    pv = _dot(jnp.concatenate(ps, axis=0), v1)
    return tuple((ms[h], alphas[h] * state[h][1] + pv[h * qn:(h + 1) * qn]) for h in range(nh))


def _nsa_body(q_ref, gl_ref, kc_ref, vc_ref, ks_ref, vs_ref, kw_ref, vw_ref, ovt_ref, o_ref):
    i = pl.program_id(2)
    hpg = NSA_GROUP
    q = q_ref[...].reshape(hpg * Q_TILE, HEAD_DIM)
    rr = lax.broadcasted_iota(jnp.int32, (Q_TILE, 1), 0)
    t_q = i * Q_TILE + rr
    heads = [slice(h * Q_TILE, (h + 1) * Q_TILE) for h in range(hpg)]

    kc = kc_ref[...]
    ncb = kc.shape[0]
    s_c = _dot_nt(q, kc)
    cmask = lax.broadcasted_iota(jnp.int32, (1, ncb), 1) * CMP_STRIDE + (CMP_LEN - 1) <= t_q
    pcs, psum = [], None
    for r in heads:
        s = jnp.where(cmask, s_c[r], NEG_INF)
        p = jnp.where(cmask, jnp.exp2(s - jnp.max(s, axis=-1, keepdims=True)), 0.0)
        l = jnp.sum(p, axis=-1, keepdims=True)
        p = p * (1.0 / jnp.where(l > 0.0, l, 1.0))
        pcs.append(p.astype(BF16))
        psum = p if psum is None else psum + p
    o_cmp = _dot(jnp.concatenate(pcs, axis=0), vc_ref[...])

    init = tuple((jnp.full((Q_TILE, 1), NEG_INF, F32), jnp.zeros((Q_TILE, LANES), F32)) for _ in range(hpg))

    ovt = ovt_ref[...]
    imp_t = sum(_dot_nt(ovt, part) for part in _split3(psum))
    ns = imp_t.shape[0]
    sblk = lax.broadcasted_iota(jnp.int32, (ns, Q_TILE), 0)
    cur = (i * Q_TILE + lax.broadcasted_iota(jnp.int32, (ns, Q_TILE), 1)) // SLC_BLOCK
    forced = (sblk == 0) | (sblk == cur) | (sblk == cur - 1)
    imp_t = jnp.where(sblk <= cur, jnp.where(forced, FORCE_SCORE, imp_t), NEG_INF)
    bias_t = _topk_block_bias(imp_t)
    bias_pad = jnp.concatenate([bias_t, jnp.zeros((LANES - ns, Q_TILE), F32)], axis=0).T
    bias = bias_pad[:, :ns].astype(BF16)
    qp = jnp.concatenate([q, jnp.concatenate([bias] * hpg, axis=0)], axis=1)

    n_full = (i * Q_TILE) // SEL_CHUNK

    def sel_step(c, st):
        k0 = pl.multiple_of(c * SEL_CHUNK, SEL_CHUNK)
        return _flash_update(qp, ks_ref[pl.ds(k0, SEL_CHUNK), :], vs_ref[pl.ds(k0, SEL_CHUNK), :], None, st)

    st_s = lax.fori_loop(0, n_full, sel_step, init)
    k0 = pl.multiple_of(n_full * SEL_CHUNK, SEL_CHUNK)
    causal = jnp.where(k0 + lax.broadcasted_iota(jnp.int32, (1, SEL_CHUNK), 1) <= t_q, 0.0, NEG_INF)
    st_s = _flash_update(qp, ks_ref[pl.ds(k0, SEL_CHUNK), :], vs_ref[pl.ds(k0, SEL_CHUNK), :], causal, st_s)

    wkeys = WINDOW + Q_TILE
    w0 = pl.multiple_of(jnp.maximum(i * Q_TILE - WINDOW, 0), Q_TILE)
    dpos = t_q - (w0 + lax.broadcasted_iota(jnp.int32, (1, wkeys), 1))
    band = jnp.where((dpos >= 0) & (dpos < WINDOW), 0.0, NEG_INF)
    st_w = _flash_update(q, kw_ref[pl.ds(w0, wkeys), :], vw_ref[pl.ds(w0, wkeys), :], band, init)

    sig = jax.nn.sigmoid(gl_ref[...])
    outs = []
    for h, r in enumerate(heads):
        c = h * N_BRANCH
        acc_s, acc_w = st_s[h][1], st_w[h][1]
        g_s = sig[:, c + 1:c + 2] / acc_s[:, HEAD_DIM:HEAD_DIM + 1]
        g_w = sig[:, c + 2:c + 3] / acc_w[:, HEAD_DIM:HEAD_DIM + 1]
        outs.append(sig[:, c:c + 1] * o_cmp[r] + g_s * acc_s[:, :HEAD_DIM] + g_w * acc_w[:, :HEAD_DIM])
    o_ref[...] = jnp.concatenate(outs, axis=1).astype(BF16)


def _nsa(hm, wd, gates, kvc, ovt, batch, seq):
    n = batch * seq
    nq = seq // Q_TILE
    ncb = kvc.shape[2]
    g_ = NSA_KV_HEADS
    return pl.pallas_call(
        _nsa_body,
        grid=(batch, g_, nq),
        in_specs=[
            pl.BlockSpec((NSA_GROUP, Q_TILE, HEAD_DIM), lambda b, g, i: (g, b * nq + i, 0)),
            pl.BlockSpec((Q_TILE, LANES), lambda b, g, i: (b * nq + i, g)),
            pl.BlockSpec((None, None, ncb, HEAD_DIM), lambda b, g, i: (g, b, 0, 0)),
            pl.BlockSpec((None, None, ncb, HEAD_DIM), lambda b, g, i: (g_ + g, b, 0, 0)),
            pl.BlockSpec((None, seq, LANES), lambda b, g, i: (WD_KS + g, b, 0)),
            pl.BlockSpec((None, seq, LANES), lambda b, g, i: (WD_VS + g, b, 0)),
            pl.BlockSpec((None, seq, HEAD_DIM), lambda b, g, i: (HM_KW + g, b, 0)),
            pl.BlockSpec((None, seq, LANES), lambda b, g, i: (WD_VW + g, b, 0)),
            pl.BlockSpec(ovt.shape, lambda b, g, i: (0, 0)),
        ],
        out_specs=pl.BlockSpec((Q_TILE, NSA_GROUP * HEAD_DIM), lambda b, g, i: (b * nq + i, g)),
        out_shape=jax.ShapeDtypeStruct((n, NSA_Q_W), BF16),
        compiler_params=_params("parallel", "parallel", "arbitrary"),
        name="nsa",
    )(hm, gates, kvc, kvc, wd, wd, hm, wd, ovt)


def _conv_body(u_ref, up_ref, w_ref, b_ref, lg_ref, lb_ref, pw_ref, o_ref, hbuf):
    i = pl.program_id(1)

    def glu(u):
        u = u.astype(F32)
        return u[:, :CONV_CH] * jax.nn.sigmoid(u[:, CONV_CH:])

    prev = glu(up_ref[...])
    hbuf[0:CONV_HALO] = jnp.where(i > 0, prev, 0.0)
    hbuf[CONV_HALO:] = glu(u_ref[...])
    w = w_ref[...]
    acc = jnp.zeros((CONV_TILE, CONV_CH), F32) + b_ref[...]
    base = CONV_HALO - (CONV_K - 1)
    for j in range(CONV_K):
        acc = acc + w[j:j + 1] * hbuf[base + j:base + j + CONV_TILE]
    mu = jnp.mean(acc, axis=-1, keepdims=True)
    var = jnp.mean(jnp.square(acc - mu), axis=-1, keepdims=True)
    hf = (acc - mu) * lax.rsqrt(var + 1e-5) * lg_ref[...] + lb_ref[...]
    hf = hf * jax.nn.sigmoid(hf)
    o_ref[...] = _dot(hf.astype(BF16), pw_ref[...]).astype(BF16)


def _conv(conv_in, w32, b, lg, lb, pw, batch, seq):
    n = batch * seq
    nt = seq // CONV_TILE
    halo_per_tile = CONV_TILE // CONV_HALO
    row = lambda v: v.reshape(1, CONV_CH)
    return pl.pallas_call(
        _conv_body,
        grid=(batch, nt),
        in_specs=[
            pl.BlockSpec((CONV_TILE, 2 * CONV_CH), lambda b_, i: (b_ * nt + i, 0)),
            pl.BlockSpec((CONV_HALO, 2 * CONV_CH),
                         lambda b_, i: (jnp.maximum((b_ * nt + i) * halo_per_tile - 1, 0), 0)),
            pl.BlockSpec((32, CONV_CH), lambda b_, i: (0, 0)),
            pl.BlockSpec((1, CONV_CH), lambda b_, i: (0, 0)),
            pl.BlockSpec((1, CONV_CH), lambda b_, i: (0, 0)),
            pl.BlockSpec((1, CONV_CH), lambda b_, i: (0, 0)),
            pl.BlockSpec((CONV_CH, CONV_CH), lambda b_, i: (0, 0)),
        ],
        out_specs=pl.BlockSpec((CONV_TILE, CONV_CH), lambda b_, i: (b_ * nt + i, 0)),
        out_shape=jax.ShapeDtypeStruct((n, CONV_CH), BF16),
        scratch_shapes=[pltpu.VMEM((CONV_HALO + CONV_TILE, CONV_CH), F32)],
        compiler_params=_params("parallel", "parallel"),
        name="conformer_conv",
    )(conv_in, conv_in, w32, row(b), row(lg), row(lb), pw)


def _ret_body(gch_ref, q_ref, k_ref, v_ref, g_ref, cos_ref, sin_ref, perm_ref, decay_ref, xi_ref, zeta_ref,
              gn_ref, o_ref, state):
    c = pl.program_id(1)

    @pl.when(c == 0)
    def _():
        state[...] = jnp.zeros_like(state)

    perm = perm_ref[...]
    cos = cos_ref[...]
    sin = sin_ref[...]
    xi = xi_ref[...]
    gn = gn_ref[...]

    def rope(x):
        return x.astype(F32) * cos + _dot(x, perm) * sin

    nb = q_ref.shape[0]
    old = [[state[bi, h] for h in range(RET_HEADS)] for bi in range(nb)]
    new_states, results = [], []
    for bi in range(nb):
        qr = rope(q_ref[bi])
        kr = rope(k_ref[bi])
        kz_t = (kr * zeta_ref[...]).T
        v = v_ref[bi]
        gate = g_ref[bi].astype(F32)
        outs = []
        for h in range(RET_HEADS):
            sl = slice(h * HEAD_DIM, (h + 1) * HEAD_DIM)
            qh = qr[:, sl].astype(BF16)
            kh = kr[:, sl].astype(BF16)
            vh = v[:, sl]
            a = _dot_nt(qh, kh) * decay_ref[h]
            s_h = old[bi][h]
            o = _dot(a.astype(BF16), vh) + _dot(qh, s_h.astype(BF16)) * xi[:, sl]
            new_states.append(gch_ref[h] * s_h + _dot(kz_t[sl].astype(BF16), vh))
            mu = jnp.mean(o, axis=-1, keepdims=True)
            var = jnp.mean(jnp.square(o - mu), axis=-1, keepdims=True)
            outs.append((o - mu) * lax.rsqrt(var + 1e-5))
        o = jnp.concatenate(outs, axis=1) * gn
        results.append((gate * jax.nn.sigmoid(gate) * o).astype(BF16))
    for bi in range(nb):
        o_ref[bi] = results[bi]
        for h in range(RET_HEADS):
            state[bi, h] = new_states[bi * RET_HEADS + h]


def _retention_tables(seq):
    half = HEAD_DIM // 2
    pos = jnp.arange(seq, dtype=F32)
    inv = ROPE_BASE ** (-jnp.arange(half, dtype=F32) / half)
    ang = pos[:, None] * inv[None, :]
    cos, sin = jnp.cos(ang), jnp.sin(ang)
    cos_t = jnp.tile(jnp.concatenate([cos, cos], axis=1), (1, RET_HEADS))
    sin_t = jnp.tile(jnp.concatenate([-sin, sin], axis=1), (1, RET_HEADS))
    log_gamma = jnp.log1p(-jnp.exp2(-5.0 - jnp.arange(RET_HEADS, dtype=F32)))
    i = jnp.arange(RET_CHUNK, dtype=F32)
    rel = i[:, None] - i[None, :]
    decay = jnp.where(rel >= 0, jnp.exp(jnp.maximum(rel, 0.0)[None] * log_gamma[:, None, None]), 0.0)
    xi = jnp.exp((i + 1.0)[None, :] * log_gamma[:, None])
    zeta = jnp.exp((RET_CHUNK - 1.0 - i)[None, :] * log_gamma[:, None])
    g_chunk = jnp.exp(RET_CHUNK * log_gamma)
    lanes = lambda a: jnp.repeat(a.T, HEAD_DIM, axis=1)
    j = np.arange(RET_W)
    partner = (j // HEAD_DIM) * HEAD_DIM + (j % HEAD_DIM + half) % HEAD_DIM
    perm = np.zeros((RET_W, RET_W), np.float32)
    perm[partner, j] = 1.0
    return cos_t, sin_t, jnp.asarray(perm, BF16), decay, lanes(xi), lanes(zeta), g_chunk


def _retention(ret, gn_g, tables, batch, seq):
    nc = seq // RET_CHUNK
    rb = RET_BATCH
    cos_t, sin_t, perm, decay, xi, zeta, g_chunk = tables
    ret3 = ret.reshape(batch, seq, 4 * RET_W)
    tile = lambda col: pl.BlockSpec((rb, RET_CHUNK, RET_W), lambda b, c: (b, c, col))
    const = lambda shape: pl.BlockSpec(shape, lambda b, c: (0,) * len(shape))
    out = pl.pallas_call(
        _ret_body,
        grid=(batch // rb, nc),
        in_specs=[
            pl.BlockSpec(memory_space=pltpu.SMEM),
            tile(0), tile(1), tile(2), tile(3),
            pl.BlockSpec((RET_CHUNK, RET_W), lambda b, c: (c, 0)),
            pl.BlockSpec((RET_CHUNK, RET_W), lambda b, c: (c, 0)),
            const((RET_W, RET_W)),
            const((RET_HEADS, RET_CHUNK, RET_CHUNK)),
            const((RET_CHUNK, RET_W)),
            const((RET_CHUNK, RET_W)),
            const((1, RET_W)),
        ],
        out_specs=pl.BlockSpec((rb, RET_CHUNK, RET_W), lambda b, c: (b, c, 0)),
        out_shape=jax.ShapeDtypeStruct((batch, seq, RET_W), BF16),
        scratch_shapes=[pltpu.VMEM((rb, RET_HEADS, HEAD_DIM, HEAD_DIM), F32)],
        compiler_params=_params("parallel", "arbitrary"),
        name="retention",
    )(g_chunk, ret3, ret3, ret3, ret3, cos_t, sin_t, perm, decay, xi, zeta, gn_g.reshape(1, RET_W))
    return out.reshape(batch * seq, RET_W)


def _out_proj_body(*refs, with_router):
    if with_router:
        oa_ref, ob_ref, oc_ref, x_ref, w_ref, g_ref, rhi_ref, rlo_ref, xo_ref, h_ref, lg_ref = refs
    else:
        oa_ref, ob_ref, oc_ref, x_ref, w_ref, g_ref, xo_ref, h_ref = refs
    y = (_dot(oa_ref[...], w_ref[0:NSA_Q_W])
         + _dot(ob_ref[...], w_ref[NSA_Q_W:NSA_Q_W + CONV_CH])
         + _dot(oc_ref[...], w_ref[NSA_Q_W + CONV_CH:]))
    x = x_ref[...] + y
    xo_ref[...] = x
    ms = jnp.mean(x * x, axis=-1, keepdims=True)
    h = x * lax.rsqrt(ms + EPS) * g_ref[...]
    hi = h.astype(BF16)
    h_ref[...] = hi
    if with_router:
        lo = (h - hi.astype(F32)).astype(BF16)
        lg_ref[...] = _dot(hi, rhi_ref[...]) + _dot(lo, rhi_ref[...]) + _dot(hi, rlo_ref[...])


def _out_proj(oa, ob, oc, x2, w_out, ln_g, router=None):
    n = x2.shape[0]
    tm = ROW_TILE
    with_router = router is not None
    row = lambda w: pl.BlockSpec((tm, w), lambda i: (i, 0))
    const = lambda shape: pl.BlockSpec(shape, lambda i: (0, 0))
    in_specs = [row(NSA_Q_W), row(CONV_CH), row(RET_W), row(D_MODEL), const((D_MODEL, D_MODEL)), const((1, D_MODEL))]
    args = [oa, ob, oc, x2, w_out.astype(BF16), ln_g.reshape(1, D_MODEL)]
    out_specs = [row(D_MODEL), row(D_MODEL)]
    out_shape = [jax.ShapeDtypeStruct((n, D_MODEL), F32), jax.ShapeDtypeStruct((n, D_MODEL), BF16)]
    if with_router:
        rpad = jnp.pad(router, ((0, 0), (0, LANES - N_EXPERTS)))
        rhi = rpad.astype(BF16)
        rlo = (rpad - rhi.astype(F32)).astype(BF16)
        in_specs += [const((D_MODEL, LANES)), const((D_MODEL, LANES))]
        args += [rhi, rlo]
        out_specs.append(row(LANES))
        out_shape.append(jax.ShapeDtypeStruct((n, LANES), F32))
    return pl.pallas_call(
        functools.partial(_out_proj_body, with_router=with_router),
        grid=(n // tm,),
        in_specs=in_specs,
        out_specs=out_specs,
        out_shape=out_shape,
        compiler_params=_params("parallel"),
        name="out_proj_router" if with_router else "out_proj",
    )(*args)


def _ffn_body(*refs, with_residual):
    if with_residual:
        e_ref, x_ref, wa_ref, wb_ref, w2_ref, r_ref, o_ref, acc = refs
    else:
        e_ref, x_ref, wa_ref, wb_ref, w2_ref, o_ref, acc = refs
    del e_ref
    f = pl.program_id(1)

    @pl.when(f == 0)
    def _():
        acc[...] = jnp.zeros_like(acc)

    x = x_ref[...]
    a = _dot(x, wa_ref[...])
    b = _dot(x, wb_ref[...])
    hmid = (a * jax.nn.sigmoid(a) * b).astype(BF16)
    acc[...] += _dot(hmid, w2_ref[...])

    @pl.when(f == pl.num_programs(1) - 1)
    def _():
        o_ref[...] = acc[...] + r_ref[...] if with_residual else acc[...]


def _ffn(xs, blk_e, w13, w2, rows, residual=None):
    p = xs.shape[0]
    nf = D_FF // FF_TILE
    with_residual = residual is not None
    in_specs = [
        pl.BlockSpec((rows, D_MODEL), lambda i, f, e: (i, 0)),
        pl.BlockSpec((None, D_MODEL, FF_TILE), lambda i, f, e: (e[i], 0, f)),
        pl.BlockSpec((None, D_MODEL, FF_TILE), lambda i, f, e: (e[i], 0, nf + f)),
        pl.BlockSpec((None, FF_TILE, D_MODEL), lambda i, f, e: (e[i], f, 0)),
    ]
    args = [xs, w13, w13, w2]
    if with_residual:
        in_specs.append(pl.BlockSpec((rows, D_MODEL), lambda i, f, e: (i, 0)))
        args.append(residual)
    return pl.pallas_call(
        functools.partial(_ffn_body, with_residual=with_residual),
        grid_spec=pltpu.PrefetchScalarGridSpec(
            num_scalar_prefetch=1,
            grid=(p // rows, nf),
            in_specs=in_specs,
            out_specs=pl.BlockSpec((rows, D_MODEL), lambda i, f, e: (i, 0)),
            scratch_shapes=[pltpu.VMEM((rows, D_MODEL), F32)],
        ),
        out_shape=jax.ShapeDtypeStruct((p, D_MODEL), F32),
        compiler_params=_params("parallel", "arbitrary"),
        name="ffn_residual" if with_residual else "ffn_grouped",
    )(blk_e, *args)


def _final_body(x_ref, y0_ref, y1_ref, gt_ref, g_ref, o_ref):
    gt = gt_ref[...]
    x = x_ref[...] + gt[:, 0:1] * y0_ref[...] + gt[:, 1:2] * y1_ref[...]
    ms = jnp.mean(x * x, axis=-1, keepdims=True)
    o_ref[...] = x * lax.rsqrt(ms + EPS) * g_ref[...]


def _final(x2, y0, y1, gates_pad, ln_g):
    n = x2.shape[0]
    tm = ROW_TILE
    row = lambda w: pl.BlockSpec((tm, w), lambda i: (i, 0))
    return pl.pallas_call(
        _final_body,
        grid=(n // tm,),
        in_specs=[row(D_MODEL), row(D_MODEL), row(D_MODEL), row(LANES), pl.BlockSpec((1, D_MODEL), lambda i: (0, 0))],
        out_specs=row(D_MODEL),
        out_shape=jax.ShapeDtypeStruct((n, D_MODEL), F32),
        compiler_params=_params("parallel"),
        name="moe_combine_final_norm",
    )(x2, y0, y1, gates_pad, ln_g.reshape(1, D_MODEL))


def _overlap_t(seq):
    nc = seq // CMP_STRIDE
    ns = seq // SLC_BLOCK
    cs = np.arange(nc)[None, :] * CMP_STRIDE
    ss = np.arange(ns)[:, None] * SLC_BLOCK
    return jnp.asarray(((cs < ss + SLC_BLOCK) & (cs + CMP_LEN > ss)).astype(np.float32), BF16)


def _route(logits, rows):
    n = logits.shape[0]
    top_logit, top_e = lax.top_k(logits, TOP_K)
    gates = jax.nn.softmax(top_logit, axis=-1)
    e_flat = top_e.reshape(-1).astype(jnp.int32)
    onehot = (e_flat[:, None] == jnp.arange(N_EXPERTS, dtype=jnp.int32)[None, :]).astype(jnp.int32)
    csum = jnp.cumsum(onehot, axis=0)
    counts = csum[-1]
    padded = ((counts + rows - 1) // rows) * rows
    starts = jnp.cumsum(counts) - counts
    pends = jnp.cumsum(padded)
    pstarts = pends - padded
    pos = jnp.sum(onehot * (csum - 1 + pstarts[None, :]), axis=1).reshape(n, TOP_K)
    p = n * TOP_K + N_EXPERTS * rows
    nblk = p // rows
    blk_e = jnp.minimum(jnp.searchsorted(pends, jnp.arange(nblk) * rows, side='right'), N_EXPERTS - 1)
    blk_e = blk_e.astype(jnp.int32)
    order = jnp.argsort(e_flat, stable=True).astype(jnp.int32)
    slot_e = jnp.repeat(blk_e, rows)
    off = jnp.arange(p, dtype=jnp.int32) - pstarts[slot_e]
    live = (off < counts[slot_e]) & (jnp.arange(p) < pends[-1])
    src = jnp.clip(starts[slot_e] + off, 0, n * TOP_K - 1)
    buf_tok = jnp.where(live, order[src] // TOP_K, n).astype(jnp.int32)
    return gates, buf_tok, pos, blk_e


def _token_mixer(x2, l, batch, seq, tables, ovt, ln_attn, w_in, cmp_pe, cmp_w1, cmp_w2, conv_w, conv_b, conv_ln_g,
                 conv_ln_b, conv_pw, ret_gn):
    hm, wd, gates, conv_in, ret = _norm_proj(x2, ln_attn[l], _arrange_w_in(w_in[l]), seq)
    nsub = seq // CMP_STRIDE
    half = CMP_STRIDE * HEAD_DIM
    sub = hm[HM_KC:HM_KW].reshape(4, batch, nsub, half)
    w1 = cmp_w1[l]
    w1cat = jnp.concatenate([w1[:, :half], w1[:, half:]], axis=2).astype(BF16)
    pe8 = jnp.broadcast_to(cmp_pe[l].reshape(2, 1, CMP_LEN * HEAD_DIM), (2, 8, CMP_LEN * HEAD_DIM)).astype(BF16)
    kvc = _compress(sub, pe8, w1cat, cmp_w2[l].astype(BF16))
    o_a = _nsa(hm, wd, gates, kvc, ovt, batch, seq)
    w32 = jnp.pad(conv_w[l], ((0, 32 - CONV_K), (0, 0)))
    o_b = _conv(conv_in, w32, conv_b[l], conv_ln_g[l], conv_ln_b[l], conv_pw[l].astype(BF16), batch, seq)
    o_c = _retention(ret, ret_gn[l], tables, batch, seq)
    return o_a, o_b, o_c


def kernel(x, ln_attn, w_in, cmp_pe, cmp_w1, cmp_w2, conv_w, conv_b, conv_ln_g, conv_ln_b, conv_pw, ret_gn, w_out,
           ln_ffn, ffn_w13, ffn_w2, router, moe_w13, moe_w2, ln_final):
    batch, seq, d = x.shape
    n = batch * seq
    depth = ln_attn.shape[0]
    x2 = x.reshape(n, d)
    tables = _retention_tables(seq)
    ovt = _overlap_t(seq)
    out = None
    for l in range(depth):
        o_a, o_b, o_c = _token_mixer(x2, l, batch, seq, tables, ovt, ln_attn, w_in, cmp_pe, cmp_w1, cmp_w2, conv_w,
                                     conv_b, conv_ln_g, conv_ln_b, conv_pw, ret_gn)
        if l % 2 == 0:
            x2, h2 = _out_proj(o_a, o_b, o_c, x2, w_out[l], ln_ffn[l])
            zeros_e = jnp.zeros((n // FFN_ROWS,), jnp.int32)
            x2 = _ffn(h2, zeros_e, ffn_w13[l // 2][None].astype(BF16), ffn_w2[l // 2][None].astype(BF16), FFN_ROWS,
                      residual=x2)
        else:
            x2, h2, logits = _out_proj(o_a, o_b, o_c, x2, w_out[l], ln_ffn[l], router=router[l // 2])
            gates, buf_tok, pos, blk_e = _route(logits[:, :N_EXPERTS], MOE_ROWS)
            h2_pad = jnp.concatenate([h2, jnp.zeros((1, d), h2.dtype)], axis=0)
            xs = h2_pad[buf_tok]
            yb = _ffn(xs, blk_e, moe_w13[l // 2].astype(BF16), moe_w2[l // 2].astype(BF16), MOE_ROWS)
            gates_pad = jnp.pad(gates, ((0, 0), (0, LANES - TOP_K)))
            if l == depth - 1:
                out = _final(x2, yb[pos[:, 0]], yb[pos[:, 1]], gates_pad, ln_final)
            else:
                x2 = x2 + gates[:, 0:1] * yb[pos[:, 0]] + gates[:, 1:2] * yb[pos[:, 1]]
    if out is None:
        out = _final(x2, jnp.zeros_like(x2), jnp.zeros_like(x2), jnp.zeros((n, LANES), F32), ln_final)
    return out.reshape(batch, seq, d)
```

```python
import functools

import numpy as np
import jax
import jax.numpy as jnp
from jax import lax
from jax.experimental import pallas as pl
from jax.experimental.pallas import tpu as pltpu

D_MODEL = 1024
HEAD_DIM = 64
NSA_HEADS = 8
NSA_KV_HEADS = 2
NSA_GROUP = NSA_HEADS // NSA_KV_HEADS
CMP_LEN = 32
CMP_STRIDE = 16
CMP_HIDDEN = 256
SLC_BLOCK = 64
SLC_TOPK = 16
WINDOW = 512
N_BRANCH = 3
CONV_CH = 256
CONV_K = 31
RET_HEADS = 4
RET_CHUNK = 128
ROPE_BASE = 10000.0
D_FF = 2816
N_EXPERTS = 8
TOP_K = 2
NEG_INF = -1e30
FORCE_SCORE = 1e9
EPS = 1e-6

NSA_Q_W = NSA_HEADS * HEAD_DIM
NSA_KV_W = NSA_KV_HEADS * HEAD_DIM
NSA_GATE_W = NSA_HEADS * N_BRANCH
RET_W = RET_HEADS * HEAD_DIM

LANES = 128
SUBLANES = 8
Q_TILE = 512
WIN_TILE = 256
SEL_CHUNK = 512
ROW_TILE = 512
RET_BATCH = 4
CONV_TILE = 512
CONV_HALO = 32
MOE_ROWS = 512
FFN_ROWS = 512
FF_TILE = 1408
VMEM_LIMIT = 56 * 1024 * 1024

BF16 = jnp.bfloat16
F32 = jnp.float32

HM_Q, HM_KC, HM_VC, HM_KW, HM_SLABS = 0, 8, 10, 12, 14
WD_KS, WD_VS, WD_VW, WD_SLABS = 0, 2, 4, 6
LOG2E = 1.4426950408889634


def _dot(a, b):
    return jnp.dot(a, b, preferred_element_type=F32)


def _dot_nt(a, b):
    return lax.dot_general(a, b, (((1,), (1,)), ((), ())), preferred_element_type=F32)


def _split3(x):
    hi = x.astype(BF16)
    r1 = x - hi.astype(F32)
    mid = r1.astype(BF16)
    lo = (r1 - mid.astype(F32)).astype(BF16)
    return hi, mid, lo


def _params(*sem):
    return pltpu.CompilerParams(dimension_semantics=sem, vmem_limit_bytes=VMEM_LIMIT)


W_HM = HM_SLABS * HEAD_DIM
W_WD = W_HM + 3 * NSA_KV_W
W_GATE = W_WD + 2 * LANES
W_CONV = W_GATE + 2 * CONV_CH
W_RET = W_CONV + 4 * RET_W


def _arrange_w_in(w_in):
    offs = np.cumsum([0, NSA_Q_W] + [NSA_KV_W] * 6 + [NSA_GATE_W, 2 * CONV_CH] + [RET_W] * 4)
    q, kc, vc, ks, vs, kw, vw, gl, conv, rq, rk, rv, rg = [w_in[:, offs[i]:offs[i + 1]] for i in range(13)]
    scale = HEAD_DIM ** -0.5
    gpad = jnp.zeros((D_MODEL, LANES - NSA_GATE_W // 2), w_in.dtype)
    gates = jnp.concatenate([gl[:, :NSA_GATE_W // 2], gpad, gl[:, NSA_GATE_W // 2:], gpad], axis=1)
    w = jnp.concatenate([q * (scale * LOG2E), kc, vc, kw, ks, vs, vw, gates, conv, rq, rk * scale, rv, rg], axis=1)
    return w.astype(BF16)


def _norm_proj_body(x_ref, g_ref, w_ref, hm_ref, wd_ref, gates_ref, conv_ref, ret_ref, *, seq):
    x = x_ref[...]
    ms = jnp.mean(x * x, axis=-1, keepdims=True)
    h = (x * lax.rsqrt(ms + EPS) * g_ref[...]).astype(BF16)
    res = _dot(h, w_ref[:, 0:W_HM])
    for j in range(HM_SLABS):
        hm_ref[j] = res[:, j * HEAD_DIM:(j + 1) * HEAD_DIM].astype(BF16)
    rows = x.shape[0]
    t0 = (pl.program_id(0) % (seq // rows)) * rows
    lane = lax.broadcasted_iota(jnp.int32, (rows, HEAD_DIM), 1)
    blk = (t0 + lax.broadcasted_iota(jnp.int32, (rows, HEAD_DIM), 0)) // SLC_BLOCK
    onehot = jnp.where(blk == lane, 1.0, 0.0)
    ones_col = jnp.where(lane == 0, 1.0, 0.0)
    wide = _dot(h, w_ref[:, W_HM:W_WD])
    for j, ext in ((WD_KS, onehot), (WD_VS, ones_col), (WD_VW, ones_col)):
        for g in range(NSA_KV_HEADS):
            c = (j + g) * HEAD_DIM
            wd_ref[j + g] = jnp.concatenate([wide[:, c:c + HEAD_DIM], ext], axis=1).astype(BF16)
    gates_ref[...] = _dot(h, w_ref[:, W_WD:W_GATE])
    conv_ref[...] = _dot(h, w_ref[:, W_GATE:W_CONV]).astype(BF16)
    ret_ref[...] = _dot(h, w_ref[:, W_CONV:W_RET]).astype(BF16)


def _norm_proj(x2, ln_g, w_arr, seq):
    n = x2.shape[0]
    tm = ROW_TILE
    return pl.pallas_call(
        functools.partial(_norm_proj_body, seq=seq),
        grid=(n // tm,),
        in_specs=[
            pl.BlockSpec((tm, D_MODEL), lambda i: (i, 0)),
            pl.BlockSpec((1, D_MODEL), lambda i: (0, 0)),
            pl.BlockSpec((D_MODEL, W_RET), lambda i: (0, 0)),
        ],
        out_specs=[
            pl.BlockSpec((HM_SLABS, tm, HEAD_DIM), lambda i: (0, i, 0)),
            pl.BlockSpec((WD_SLABS, tm, LANES), lambda i: (0, i, 0)),
            pl.BlockSpec((tm, 2 * LANES), lambda i: (i, 0)),
            pl.BlockSpec((tm, 2 * CONV_CH), lambda i: (i, 0)),
            pl.BlockSpec((tm, 4 * RET_W), lambda i: (i, 0)),
        ],
        out_shape=[
            jax.ShapeDtypeStruct((HM_SLABS, n, HEAD_DIM), BF16),
            jax.ShapeDtypeStruct((WD_SLABS, n, LANES), BF16),
            jax.ShapeDtypeStruct((n, 2 * LANES), F32),
            jax.ShapeDtypeStruct((n, 2 * CONV_CH), BF16),
            jax.ShapeDtypeStruct((n, 4 * RET_W), BF16),
        ],
        compiler_params=_params("parallel"),
        name="norm_proj",
    )(x2, ln_g.reshape(1, D_MODEL), w_arr)


def _compress_body(s_ref, pe_ref, w1_ref, w2_ref, o_ref):
    u = _dot(s_ref[...], w1_ref[...])
    pe = pe_ref[...]
    bias = (_dot(pe[:, :CMP_STRIDE * HEAD_DIM], w1_ref[:, :CMP_HIDDEN])
            + _dot(pe[:, CMP_STRIDE * HEAD_DIM:], w1_ref[:, CMP_HIDDEN:]))[0:1]
    bot = u[:, CMP_HIDDEN:]
    bot_next = jnp.concatenate([bot[1:], bot[:1]], axis=0)
    hid = u[:, :CMP_HIDDEN] + bot_next + bias
    hid = hid * jax.nn.sigmoid(hid)
    o_ref[...] = _dot(hid.astype(BF16), w2_ref[...]).astype(BF16)


def _compress(sub, pe8, w1cat, w2):
    _, b, nsub, width = sub.shape
    return pl.pallas_call(
        _compress_body,
        grid=(4, b),
        in_specs=[
            pl.BlockSpec((None, None, nsub, width), lambda j, i: (j, i, 0, 0)),
            pl.BlockSpec((None, 8, 2 * width), lambda j, i: (j // 2, 0, 0)),
            pl.BlockSpec((None, width, 2 * CMP_HIDDEN), lambda j, i: (j // 2, 0, 0)),
            pl.BlockSpec((None, CMP_HIDDEN, HEAD_DIM), lambda j, i: (j // 2, 0, 0)),
        ],
        out_specs=pl.BlockSpec((None, None, nsub, HEAD_DIM), lambda j, i: (j, i, 0, 0)),
        out_shape=jax.ShapeDtypeStruct((4, b, nsub, HEAD_DIM), BF16),
        compiler_params=_params("parallel", "parallel"),
        name="compress",
    )(sub, pe8, w1cat, w2)


def _topk_block_bias(imp_t):
    ns, nq = imp_t.shape
    sub = 8
    nb = ns // sub
    blocks = [imp_t[v * sub:(v + 1) * sub] for v in range(nb)]
    cnts = [jnp.zeros((sub, nq), jnp.int32) for _ in range(nb)]
    sub_iota = lax.broadcasted_iota(jnp.int32, (sub, nq), 0)
    for i in range(ns):
        row = jnp.broadcast_to(imp_t[i:i + 1], (sub, nq))
        for v in range(nb):
            if v > i // sub:
                cnts[v] = cnts[v] + jnp.where(row >= blocks[v], 1, 0)
            elif v < i // sub:
                cnts[v] = cnts[v] + jnp.where(row > blocks[v], 1, 0)
            else:
                tie = (row == blocks[v]) & (sub_iota > i % sub)
                cnts[v] = cnts[v] + jnp.where(row > blocks[v], 1, 0) + jnp.where(tie, 1, 0)
    cnt = jnp.concatenate(cnts, axis=0)
    return jnp.where(cnt < SLC_TOPK, 0.0, NEG_INF)


def _flash_update(qs, k, v1, bias, state):
    nh = len(state)
    qn = qs.shape[0] // nh
    s_all = _dot_nt(qs, k)
    ps, ms, alphas = [], [], []
    for h in range(nh):
        s = s_all[h * qn:(h + 1) * qn]
        if bias is not None:
            s = s + bias
        m = state[h][0]
        m_new = jnp.maximum(m, jnp.max(s, axis=-1, keepdims=True))
        ps.append(jnp.exp2(s - m_new).astype(BF16))
        alphas.append(jnp.exp2(m - m_new))
        ms.append(m_new)
    pv = _dot(jnp.concatenate(ps, axis=0), v1)
    return tuple((ms[h], alphas[h] * state[h][1] + pv[h * qn:(h + 1) * qn]) for h in range(nh))


def _nsa_body(q_ref, gl_ref, kc_ref, vc_ref, ks_ref, vs_ref, kw_ref, vw_ref, ovt_ref, o_ref):
    i = pl.program_id(2)
    hpg = NSA_GROUP
    q = q_ref[...].reshape(hpg * Q_TILE, HEAD_DIM)
    rr = lax.broadcasted_iota(jnp.int32, (Q_TILE, 1), 0)
    t_q = i * Q_TILE + rr
    heads = [slice(h * Q_TILE, (h + 1) * Q_TILE) for h in range(hpg)]

    kc = kc_ref[...]
    ncb = kc.shape[0]
    s_c = _dot_nt(q, kc)
    cmask = lax.broadcasted_iota(jnp.int32, (1, ncb), 1) * CMP_STRIDE + (CMP_LEN - 1) <= t_q
    pcs, psum = [], None
    for r in heads:
        s = jnp.where(cmask, s_c[r], NEG_INF)
        p = jnp.where(cmask, jnp.exp2(s - jnp.max(s, axis=-1, keepdims=True)), 0.0)
        l = jnp.sum(p, axis=-1, keepdims=True)
        p = p * (1.0 / jnp.where(l > 0.0, l, 1.0))
        pcs.append(p.astype(BF16))
        psum = p if psum is None else psum + p
    o_cmp = _dot(jnp.concatenate(pcs, axis=0), vc_ref[...])

    init = tuple((jnp.full((Q_TILE, 1), NEG_INF, F32), jnp.zeros((Q_TILE, LANES), F32)) for _ in range(hpg))

    ovt = ovt_ref[...]
    imp_t = sum(_dot_nt(ovt, part) for part in _split3(psum))
    ns = imp_t.shape[0]
    sblk = lax.broadcasted_iota(jnp.int32, (ns, Q_TILE), 0)
    cur = (i * Q_TILE + lax.broadcasted_iota(jnp.int32, (ns, Q_TILE), 1)) // SLC_BLOCK
    forced = (sblk == 0) | (sblk == cur) | (sblk == cur - 1)
    imp_t = jnp.where(sblk <= cur, jnp.where(forced, FORCE_SCORE, imp_t), NEG_INF)
    bias_t = _topk_block_bias(imp_t)
    bias_pad = jnp.concatenate([bias_t, jnp.zeros((LANES - ns, Q_TILE), F32)], axis=0).T
    bias = bias_pad[:, :ns].astype(BF16)
    qp = jnp.concatenate([q, jnp.concatenate([bias] * hpg, axis=0)], axis=1)

    n_full = (i * Q_TILE) // SEL_CHUNK

    def sel_step(c, st):
        k0 = pl.multiple_of(c * SEL_CHUNK, SEL_CHUNK)
        return _flash_update(qp, ks_ref[pl.ds(k0, SEL_CHUNK), :], vs_ref[pl.ds(k0, SEL_CHUNK), :], None, st)

    st_s = lax.fori_loop(0, n_full, sel_step, init)
    k0 = pl.multiple_of(n_full * SEL_CHUNK, SEL_CHUNK)
    causal = jnp.where(k0 + lax.broadcasted_iota(jnp.int32, (1, SEL_CHUNK), 1) <= t_q, 0.0, NEG_INF)
    st_s = _flash_update(qp, ks_ref[pl.ds(k0, SEL_CHUNK), :], vs_ref[pl.ds(k0, SEL_CHUNK), :], causal, st_s)

    wkeys = WINDOW + WIN_TILE
    init_w = tuple((jnp.full((WIN_TILE, 1), NEG_INF, F32), jnp.zeros((WIN_TILE, LANES), F32)) for _ in range(hpg))
    st_w = []
    for u in range(Q_TILE // WIN_TILE):
        sub = slice(u * WIN_TILE, (u + 1) * WIN_TILE)
        q_u = jnp.concatenate([q[r][sub] for r in heads], axis=0)
        w0 = pl.multiple_of(jnp.maximum(i * Q_TILE + u * WIN_TILE - WINDOW, 0), WIN_TILE)
        dpos = t_q[sub] - (w0 + lax.broadcasted_iota(jnp.int32, (1, wkeys), 1))
        band = jnp.where((dpos >= 0) & (dpos < WINDOW), 0.0, NEG_INF)
        st_w.append(_flash_update(q_u, kw_ref[pl.ds(w0, wkeys), :], vw_ref[pl.ds(w0, wkeys), :], band, init_w))

    sig = jax.nn.sigmoid(gl_ref[...])
    outs = []
    for h, r in enumerate(heads):
        c = h * N_BRANCH
        acc_s = st_s[h][1]
        acc_w = jnp.concatenate([st_u[h][1] for st_u in st_w], axis=0)
        g_s = sig[:, c + 1:c + 2] / acc_s[:, HEAD_DIM:HEAD_DIM + 1]
        g_w = sig[:, c + 2:c + 3] / acc_w[:, HEAD_DIM:HEAD_DIM + 1]
        outs.append(sig[:, c:c + 1] * o_cmp[r] + g_s * acc_s[:, :HEAD_DIM] + g_w * acc_w[:, :HEAD_DIM])
    o_ref[...] = jnp.concatenate(outs, axis=1).astype(BF16)


def _nsa(hm, wd, gates, kvc, ovt, batch, seq):
    n = batch * seq
    nq = seq // Q_TILE
    ncb = kvc.shape[2]
    g_ = NSA_KV_HEADS
    return pl.pallas_call(
        _nsa_body,
        grid=(batch, g_, nq),
        in_specs=[
            pl.BlockSpec((NSA_GROUP, Q_TILE, HEAD_DIM), lambda b, g, i: (g, b * nq + i, 0)),
            pl.BlockSpec((Q_TILE, LANES), lambda b, g, i: (b * nq + i, g)),
            pl.BlockSpec((None, None, ncb, HEAD_DIM), lambda b, g, i: (g, b, 0, 0)),
            pl.BlockSpec((None, None, ncb, HEAD_DIM), lambda b, g, i: (g_ + g, b, 0, 0)),
            pl.BlockSpec((None, seq, LANES), lambda b, g, i: (WD_KS + g, b, 0)),
            pl.BlockSpec((None, seq, LANES), lambda b, g, i: (WD_VS + g, b, 0)),
            pl.BlockSpec((None, seq, HEAD_DIM), lambda b, g, i: (HM_KW + g, b, 0)),
            pl.BlockSpec((None, seq, LANES), lambda b, g, i: (WD_VW + g, b, 0)),
            pl.BlockSpec(ovt.shape, lambda b, g, i: (0, 0)),
        ],
        out_specs=pl.BlockSpec((Q_TILE, NSA_GROUP * HEAD_DIM), lambda b, g, i: (b * nq + i, g)),
        out_shape=jax.ShapeDtypeStruct((n, NSA_Q_W), BF16),
        compiler_params=_params("parallel", "parallel", "arbitrary"),
        name="nsa",
    )(hm, gates, kvc, kvc, wd, wd, hm, wd, ovt)


def _conv_body(u_ref, up_ref, w_ref, b_ref, lg_ref, lb_ref, pw_ref, o_ref, hbuf, sbuf):
    i = pl.program_id(1)

    def glu(u):
        u = u.astype(F32)
        return u[:, :CONV_CH] * jax.nn.sigmoid(u[:, CONV_CH:])

    prev = glu(up_ref[...])
    hbuf[0:CONV_HALO] = jnp.where(i > 0, prev, 0.0)
    hbuf[CONV_HALO:] = glu(u_ref[...])
    span = sbuf.shape[1]
    for s in range(1, SUBLANES):
        sbuf[s - 1] = hbuf[s:s + span]
    w = w_ref[...]
    acc = jnp.zeros((CONV_TILE, CONV_CH), F32) + b_ref[...]
    base = CONV_HALO - (CONV_K - 1)
    for j in range(CONV_K):
        s = (base + j) % SUBLANES
        a8 = base + j - s
        src = hbuf[a8:a8 + CONV_TILE] if s == 0 else sbuf[s - 1, a8:a8 + CONV_TILE]
        acc = acc + w[j:j + 1] * src
    mu = jnp.mean(acc, axis=-1, keepdims=True)
    var = jnp.mean(jnp.square(acc - mu), axis=-1, keepdims=True)
    hf = (acc - mu) * lax.rsqrt(var + 1e-5) * lg_ref[...] + lb_ref[...]
    hf = hf * jax.nn.sigmoid(hf)
    o_ref[...] = _dot(hf.astype(BF16), pw_ref[...]).astype(BF16)


def _conv(conv_in, w32, b, lg, lb, pw, batch, seq):
    n = batch * seq
    nt = seq // CONV_TILE
    halo_per_tile = CONV_TILE // CONV_HALO
    row = lambda v: v.reshape(1, CONV_CH)
    return pl.pallas_call(
        _conv_body,
        grid=(batch, nt),
        in_specs=[
            pl.BlockSpec((CONV_TILE, 2 * CONV_CH), lambda b_, i: (b_ * nt + i, 0)),
            pl.BlockSpec((CONV_HALO, 2 * CONV_CH),
                         lambda b_, i: (jnp.maximum((b_ * nt + i) * halo_per_tile - 1, 0), 0)),
            pl.BlockSpec((32, CONV_CH), lambda b_, i: (0, 0)),
            pl.BlockSpec((1, CONV_CH), lambda b_, i: (0, 0)),
            pl.BlockSpec((1, CONV_CH), lambda b_, i: (0, 0)),
            pl.BlockSpec((1, CONV_CH), lambda b_, i: (0, 0)),
            pl.BlockSpec((CONV_CH, CONV_CH), lambda b_, i: (0, 0)),
        ],
        out_specs=pl.BlockSpec((CONV_TILE, CONV_CH), lambda b_, i: (b_ * nt + i, 0)),
        out_shape=jax.ShapeDtypeStruct((n, CONV_CH), BF16),
        scratch_shapes=[pltpu.VMEM((CONV_HALO + CONV_TILE, CONV_CH), F32),
                        pltpu.VMEM((SUBLANES - 1, CONV_HALO + CONV_TILE - SUBLANES, CONV_CH), F32)],
        compiler_params=_params("parallel", "parallel"),
        name="conformer_conv",
    )(conv_in, conv_in, w32, row(b), row(lg), row(lb), pw)


def _ret_body(q_ref, k_ref, v_ref, g_ref, cos_ref, sin_ref, perm_ref, decay_ref, xi_ref, zeta_ref, maskk_ref,
              maskv_ref, gmat_ref, masks_ref, mavg_ref, gn_ref, o_ref, state):
    c = pl.program_id(1)

    @pl.when(c == 0)
    def _():
        state[...] = jnp.zeros_like(state)

    nb = q_ref.shape[0]
    chunk = q_ref.shape[1]
    cos, sin = cos_ref[...], sin_ref[...]
    qk = jnp.concatenate([q_ref[bi] for bi in range(nb)] + [k_ref[bi] for bi in range(nb)], axis=0)
    rot = _dot(qk, perm_ref[...])
    outs, new_states = [], []
    for bi in range(nb):
        qs, ks = slice(bi * chunk, (bi + 1) * chunk), slice((nb + bi) * chunk, (nb + bi + 1) * chunk)
        qb = (q_ref[bi].astype(F32) * cos + rot[qs] * sin).astype(BF16)
        kr_t = (k_ref[bi].astype(F32) * cos + rot[ks] * sin).T
        k_bd = jnp.concatenate([kr_t.astype(BF16)] * RET_HEADS, axis=1) * maskk_ref[...]
        a = _dot(qb, k_bd) * decay_ref[...]
        v = v_ref[bi]
        v_bd = jnp.concatenate([v] * RET_HEADS, axis=0) * maskv_ref[...]
        s_bd = state[bi]
        outs.append(_dot(a.astype(BF16), v_bd) + _dot(qb, s_bd.astype(BF16)) * xi_ref[...])
        kz_t = (kr_t * zeta_ref[...]).astype(BF16)
        new_states.append(gmat_ref[...] * s_bd + _dot(kz_t, v) * masks_ref[...])
    o = jnp.concatenate(outs, axis=0)

    def head_mean(x):
        hi = x.astype(BF16)
        lo = (x - hi.astype(F32)).astype(BF16)
        return _dot(hi, mavg_ref[...]) + _dot(lo, mavg_ref[...])

    d = o - head_mean(o)
    y = d * lax.rsqrt(head_mean(d * d) + 1e-5) * gn_ref[...]
    for bi in range(nb):
        gate = g_ref[bi].astype(F32)
        o_ref[bi] = (gate * jax.nn.sigmoid(gate) * y[bi * chunk:(bi + 1) * chunk]).astype(BF16)
        state[bi] = new_states[bi]


def _retention_tables(seq):
    half = HEAD_DIM // 2
    pos = jnp.arange(seq, dtype=F32)
    inv = ROPE_BASE ** (-jnp.arange(half, dtype=F32) / half)
    ang = pos[:, None] * inv[None, :]
    cos, sin = jnp.cos(ang), jnp.sin(ang)
    cos_t = jnp.tile(jnp.concatenate([cos, cos], axis=1), (1, RET_HEADS))
    sin_t = jnp.tile(jnp.concatenate([-sin, sin], axis=1), (1, RET_HEADS))
    log_gamma = jnp.log1p(-jnp.exp2(-5.0 - jnp.arange(RET_HEADS, dtype=F32)))
    i = jnp.arange(RET_CHUNK, dtype=F32)
    rel = i[:, None] - i[None, :]
    decay = jnp.where(rel >= 0, jnp.exp(jnp.maximum(rel, 0.0)[None] * log_gamma[:, None, None]), 0.0)
    decay_cat = decay.transpose(1, 0, 2).reshape(RET_CHUNK, RET_HEADS * RET_CHUNK)
    xi = jnp.exp((i + 1.0)[None, :] * log_gamma[:, None])
    zeta = jnp.exp((RET_CHUNK - 1.0 - i)[None, :] * log_gamma[:, None])
    g_chunk = jnp.exp(RET_CHUNK * log_gamma)
    lanes = lambda a: jnp.repeat(a.T, HEAD_DIM, axis=1)
    j = np.arange(RET_W)
    partner = (j // HEAD_DIM) * HEAD_DIM + (j % HEAD_DIM + half) % HEAD_DIM
    perm = np.zeros((RET_W, RET_W), np.float32)
    perm[partner, j] = 1.0
    head_of_feat = j // HEAD_DIM
    head_of_key = np.arange(RET_HEADS * RET_CHUNK) // RET_CHUNK
    maskk = (head_of_feat[:, None] == head_of_key[None, :]).astype(np.float32)
    masks = (head_of_feat[:, None] == head_of_feat[None, :]).astype(np.float32)
    gmat = jnp.asarray(masks) * jnp.repeat(g_chunk, HEAD_DIM)[:, None]
    return (cos_t, sin_t, jnp.asarray(perm, BF16), decay_cat, lanes(xi), lanes(zeta).T, jnp.asarray(maskk, BF16),
            jnp.asarray(maskk.T, BF16), gmat, jnp.asarray(masks), jnp.asarray(masks / HEAD_DIM, BF16))


def _retention(ret, gn_g, tables, batch, seq):
    nc = seq // RET_CHUNK
    rb = RET_BATCH
    cos_t, sin_t = tables[:2]
    consts = tables[2:]
    ret3 = ret.reshape(batch, seq, 4 * RET_W)
    tile = lambda col: pl.BlockSpec((rb, RET_CHUNK, RET_W), lambda b, c: (b, c, col))
    const = lambda a: pl.BlockSpec(a.shape, lambda b, c: (0,) * a.ndim)
    out = pl.pallas_call(
        _ret_body,
        grid=(batch // rb, nc),
        in_specs=[
            tile(0), tile(1), tile(2), tile(3),
            pl.BlockSpec((RET_CHUNK, RET_W), lambda b, c: (c, 0)),
            pl.BlockSpec((RET_CHUNK, RET_W), lambda b, c: (c, 0)),
            *[const(a) for a in consts],
            pl.BlockSpec((1, RET_W), lambda b, c: (0, 0)),
        ],
        out_specs=pl.BlockSpec((rb, RET_CHUNK, RET_W), lambda b, c: (b, c, 0)),
        out_shape=jax.ShapeDtypeStruct((batch, seq, RET_W), BF16),
        scratch_shapes=[pltpu.VMEM((rb, RET_W, RET_W), F32)],
        compiler_params=_params("parallel", "arbitrary"),
        name="retention",
    )(ret3, ret3, ret3, ret3, cos_t, sin_t, *consts, gn_g.reshape(1, RET_W))
    return out.reshape(batch * seq, RET_W)


def _out_proj_body(*refs, with_router):
    if with_router:
        oa_ref, ob_ref, oc_ref, x_ref, w_ref, g_ref, rhi_ref, rlo_ref, xo_ref, h_ref, lg_ref = refs
    else:
        oa_ref, ob_ref, oc_ref, x_ref, w_ref, g_ref, xo_ref, h_ref = refs
    y = (_dot(oa_ref[...], w_ref[0:NSA_Q_W])
         + _dot(ob_ref[...], w_ref[NSA_Q_W:NSA_Q_W + CONV_CH])
         + _dot(oc_ref[...], w_ref[NSA_Q_W + CONV_CH:]))
    x = x_ref[...] + y
    xo_ref[...] = x
    ms = jnp.mean(x * x, axis=-1, keepdims=True)
    h = x * lax.rsqrt(ms + EPS) * g_ref[...]
    hi = h.astype(BF16)
    h_ref[...] = hi
    if with_router:
        lo = (h - hi.astype(F32)).astype(BF16)
        lg_ref[...] = _dot(hi, rhi_ref[...]) + _dot(lo, rhi_ref[...]) + _dot(hi, rlo_ref[...])


def _out_proj(oa, ob, oc, x2, w_out, ln_g, router=None):
    n = x2.shape[0]
    tm = ROW_TILE
    with_router = router is not None
    row = lambda w: pl.BlockSpec((tm, w), lambda i: (i, 0))
    const = lambda shape: pl.BlockSpec(shape, lambda i: (0, 0))
    in_specs = [row(NSA_Q_W), row(CONV_CH), row(RET_W), row(D_MODEL), const((D_MODEL, D_MODEL)), const((1, D_MODEL))]
    args = [oa, ob, oc, x2, w_out.astype(BF16), ln_g.reshape(1, D_MODEL)]
    out_specs = [row(D_MODEL), row(D_MODEL)]
    out_shape = [jax.ShapeDtypeStruct((n, D_MODEL), F32), jax.ShapeDtypeStruct((n, D_MODEL), BF16)]
    if with_router:
        rpad = jnp.pad(router, ((0, 0), (0, LANES - N_EXPERTS)))
        rhi = rpad.astype(BF16)
        rlo = (rpad - rhi.astype(F32)).astype(BF16)
        in_specs += [const((D_MODEL, LANES)), const((D_MODEL, LANES))]
        args += [rhi, rlo]
        out_specs.append(row(LANES))
        out_shape.append(jax.ShapeDtypeStruct((n, LANES), F32))
    return pl.pallas_call(
        functools.partial(_out_proj_body, with_router=with_router),
        grid=(n // tm,),
        in_specs=in_specs,
        out_specs=out_specs,
        out_shape=out_shape,
        compiler_params=_params("parallel"),
        name="out_proj_router" if with_router else "out_proj",
    )(*args)


def _ffn_body(*refs, with_residual):
    if with_residual:
        e_ref, x_ref, wa_ref, wb_ref, w2_ref, r_ref, o_ref, acc = refs
    else:
        e_ref, x_ref, wa_ref, wb_ref, w2_ref, o_ref, acc = refs
    del e_ref
    f = pl.program_id(1)

    @pl.when(f == 0)
    def _():
        acc[...] = jnp.zeros_like(acc)

    x = x_ref[...]
    a = _dot(x, wa_ref[...])
    b = _dot(x, wb_ref[...])
    hmid = (a * jax.nn.sigmoid(a) * b).astype(BF16)
    acc[...] += _dot(hmid, w2_ref[...])

    @pl.when(f == pl.num_programs(1) - 1)
    def _():
        o_ref[...] = acc[...] + r_ref[...] if with_residual else acc[...]


def _ffn(xs, blk_e, w13, w2, rows, residual=None):
    p = xs.shape[0]
    nf = D_FF // FF_TILE
    with_residual = residual is not None
    in_specs = [
        pl.BlockSpec((rows, D_MODEL), lambda i, f, e: (i, 0)),
        pl.BlockSpec((None, D_MODEL, FF_TILE), lambda i, f, e: (e[i], 0, f)),
        pl.BlockSpec((None, D_MODEL, FF_TILE), lambda i, f, e: (e[i], 0, nf + f)),
        pl.BlockSpec((None, FF_TILE, D_MODEL), lambda i, f, e: (e[i], f, 0)),
    ]
    args = [xs, w13, w13, w2]
    if with_residual:
        in_specs.append(pl.BlockSpec((rows, D_MODEL), lambda i, f, e: (i, 0)))
        args.append(residual)
    return pl.pallas_call(
        functools.partial(_ffn_body, with_residual=with_residual),
        grid_spec=pltpu.PrefetchScalarGridSpec(
            num_scalar_prefetch=1,
            grid=(p // rows, nf),
            in_specs=in_specs,
            out_specs=pl.BlockSpec((rows, D_MODEL), lambda i, f, e: (i, 0)),
            scratch_shapes=[pltpu.VMEM((rows, D_MODEL), F32)],
        ),
        out_shape=jax.ShapeDtypeStruct((p, D_MODEL), F32),
        compiler_params=_params("parallel", "arbitrary"),
        name="ffn_residual" if with_residual else "ffn_grouped",
    )(blk_e, *args)


def _final_body(x_ref, y0_ref, y1_ref, gt_ref, g_ref, o_ref):
    gt = gt_ref[...]
    x = x_ref[...] + gt[:, 0:1] * y0_ref[...] + gt[:, 1:2] * y1_ref[...]
    ms = jnp.mean(x * x, axis=-1, keepdims=True)
    o_ref[...] = x * lax.rsqrt(ms + EPS) * g_ref[...]


def _final(x2, y0, y1, gates_pad, ln_g):
    n = x2.shape[0]
    tm = ROW_TILE
    row = lambda w: pl.BlockSpec((tm, w), lambda i: (i, 0))
    return pl.pallas_call(
        _final_body,
        grid=(n // tm,),
        in_specs=[row(D_MODEL), row(D_MODEL), row(D_MODEL), row(LANES), pl.BlockSpec((1, D_MODEL), lambda i: (0, 0))],
        out_specs=row(D_MODEL),
        out_shape=jax.ShapeDtypeStruct((n, D_MODEL), F32),
        compiler_params=_params("parallel"),
        name="moe_combine_final_norm",
    )(x2, y0, y1, gates_pad, ln_g.reshape(1, D_MODEL))


def _overlap_t(seq):
    nc = seq // CMP_STRIDE
    ns = seq // SLC_BLOCK
    cs = np.arange(nc)[None, :] * CMP_STRIDE
    ss = np.arange(ns)[:, None] * SLC_BLOCK
    return jnp.asarray(((cs < ss + SLC_BLOCK) & (cs + CMP_LEN > ss)).astype(np.float32), BF16)


def _route(logits, rows):
    n = logits.shape[0]
    top_logit, top_e = lax.top_k(logits, TOP_K)
    gates = jax.nn.softmax(top_logit, axis=-1)
    e_flat = top_e.reshape(-1).astype(jnp.int32)
    onehot = (e_flat[:, None] == jnp.arange(N_EXPERTS, dtype=jnp.int32)[None, :]).astype(jnp.int32)
    csum = jnp.cumsum(onehot, axis=0)
    counts = csum[-1]
    padded = ((counts + rows - 1) // rows) * rows
    starts = jnp.cumsum(counts) - counts
    pends = jnp.cumsum(padded)
    pstarts = pends - padded
    pos = jnp.sum(onehot * (csum - 1 + pstarts[None, :]), axis=1).reshape(n, TOP_K)
    p = n * TOP_K + N_EXPERTS * rows
    nblk = p // rows
    blk_e = jnp.minimum(jnp.searchsorted(pends, jnp.arange(nblk) * rows, side='right'), N_EXPERTS - 1)
    blk_e = blk_e.astype(jnp.int32)
    order = jnp.argsort(e_flat, stable=True).astype(jnp.int32)
    slot_e = jnp.repeat(blk_e, rows)
    off = jnp.arange(p, dtype=jnp.int32) - pstarts[slot_e]
    live = (off < counts[slot_e]) & (jnp.arange(p) < pends[-1])
    src = jnp.clip(starts[slot_e] + off, 0, n * TOP_K - 1)
    buf_tok = jnp.where(live, order[src] // TOP_K, 0).astype(jnp.int32)
    return gates, buf_tok, pos, blk_e


def _token_mixer(x2, l, batch, seq, tables, ovt, ln_attn, w_in, cmp_pe, cmp_w1, cmp_w2, conv_w, conv_b, conv_ln_g,
                 conv_ln_b, conv_pw, ret_gn):
    hm, wd, gates, conv_in, ret = _norm_proj(x2, ln_attn[l], _arrange_w_in(w_in[l]), seq)
    nsub = seq // CMP_STRIDE
    half = CMP_STRIDE * HEAD_DIM
    sub = hm[HM_KC:HM_KW].reshape(4, batch, nsub, half)
    w1 = cmp_w1[l]
    w1cat = jnp.concatenate([w1[:, :half], w1[:, half:]], axis=2).astype(BF16)
    pe8 = jnp.broadcast_to(cmp_pe[l].reshape(2, 1, CMP_LEN * HEAD_DIM), (2, 8, CMP_LEN * HEAD_DIM)).astype(BF16)
    kvc = _compress(sub, pe8, w1cat, cmp_w2[l].astype(BF16))
    o_a = _nsa(hm, wd, gates, kvc, ovt, batch, seq)
    w32 = jnp.pad(conv_w[l], ((0, 32 - CONV_K), (0, 0)))
    o_b = _conv(conv_in, w32, conv_b[l], conv_ln_g[l], conv_ln_b[l], conv_pw[l].astype(BF16), batch, seq)
    o_c = _retention(ret, ret_gn[l], tables, batch, seq)
    return o_a, o_b, o_c


def kernel(x, ln_attn, w_in, cmp_pe, cmp_w1, cmp_w2, conv_w, conv_b, conv_ln_g, conv_ln_b, conv_pw, ret_gn, w_out,
           ln_ffn, ffn_w13, ffn_w2, router, moe_w13, moe_w2, ln_final):
    batch, seq, d = x.shape
    n = batch * seq
    depth = ln_attn.shape[0]
    x2 = x.reshape(n, d)
    tables = _retention_tables(seq)
    ovt = _overlap_t(seq)
    out = None
    for l in range(depth):
        o_a, o_b, o_c = _token_mixer(x2, l, batch, seq, tables, ovt, ln_attn, w_in, cmp_pe, cmp_w1, cmp_w2, conv_w,
                                     conv_b, conv_ln_g, conv_ln_b, conv_pw, ret_gn)
        if l % 2 == 0:
            x2, h2 = _out_proj(o_a, o_b, o_c, x2, w_out[l], ln_ffn[l])
            zeros_e = jnp.zeros((n // FFN_ROWS,), jnp.int32)
            x2 = _ffn(h2, zeros_e, ffn_w13[l // 2][None].astype(BF16), ffn_w2[l // 2][None].astype(BF16), FFN_ROWS,
                      residual=x2)
        else:
            x2, h2, logits = _out_proj(o_a, o_b, o_c, x2, w_out[l], ln_ffn[l], router=router[l // 2])
            gates, buf_tok, pos, blk_e = _route(logits[:, :N_EXPERTS], MOE_ROWS)
            xs = h2[buf_tok]
            yb = _ffn(xs, blk_e, moe_w13[l // 2].astype(BF16), moe_w2[l // 2].astype(BF16), MOE_ROWS)
            gates_pad = jnp.pad(gates, ((0, 0), (0, LANES - TOP_K)))
            if l == depth - 1:
                out = _final(x2, yb[pos[:, 0]], yb[pos[:, 1]], gates_pad, ln_final)
            else:
                x2 = x2 + gates[:, 0:1] * yb[pos[:, 0]] + gates[:, 1:2] * yb[pos[:, 1]]
    if out is None:
        out = _final(x2, jnp.zeros_like(x2), jnp.zeros_like(x2), jnp.zeros((n, LANES), F32), ln_final)
    return out.reshape(batch, seq, d)
```

```python
import functools

import numpy as np
import jax
import jax.numpy as jnp
from jax import lax
from jax.experimental import pallas as pl
from jax.experimental.pallas import tpu as pltpu

D_MODEL = 1024
HEAD_DIM = 64
NSA_HEADS = 8
NSA_KV_HEADS = 2
NSA_GROUP = NSA_HEADS // NSA_KV_HEADS
CMP_LEN = 32
CMP_STRIDE = 16
CMP_HIDDEN = 256
SLC_BLOCK = 64
SLC_TOPK = 16
WINDOW = 512
N_BRANCH = 3
CONV_CH = 256
CONV_K = 31
RET_HEADS = 4
RET_CHUNK = 128
ROPE_BASE = 10000.0
D_FF = 2816
N_EXPERTS = 8
TOP_K = 2
NEG_INF = -1e30
FORCE_SCORE = 1e9
EPS = 1e-6

NSA_Q_W = NSA_HEADS * HEAD_DIM
NSA_KV_W = NSA_KV_HEADS * HEAD_DIM
NSA_GATE_W = NSA_HEADS * N_BRANCH
RET_W = RET_HEADS * HEAD_DIM

LANES = 128
SUBLANES = 8
Q_TILE = 512
WIN_TILE = 256
ROW_BLK = 32
SEL_CHUNK = 512
ROW_TILE = 512
RET_BATCH = 4
CONV_TILE = 512
CONV_HALO = 32
MOE_ROWS = 512
FFN_ROWS = 512
FF_TILE = 1408
VMEM_LIMIT = 56 * 1024 * 1024

BF16 = jnp.bfloat16
F32 = jnp.float32

HM_Q, HM_KW, HM_SLABS = 0, 8, 10
CK_SLABS = 4
WD_KS, WD_VS, WD_VW, WD_SLABS = 0, 2, 4, 6
LOG2E = 1.4426950408889634


def _dot(a, b):
    return jnp.dot(a, b, preferred_element_type=F32)


def _dot_nt(a, b):
    return lax.dot_general(a, b, (((1,), (1,)), ((), ())), preferred_element_type=F32)


def _split3(x):
    hi = x.astype(BF16)
    r1 = x - hi.astype(F32)
    mid = r1.astype(BF16)
    lo = (r1 - mid.astype(F32)).astype(BF16)
    return hi, mid, lo


def _params(*sem):
    return pltpu.CompilerParams(dimension_semantics=sem, vmem_limit_bytes=VMEM_LIMIT)


W_HM = (HM_SLABS + CK_SLABS) * HEAD_DIM
W_WD = W_HM + 3 * NSA_KV_W
W_GATE = W_WD + 2 * LANES
W_CONV = W_GATE + 2 * CONV_CH
W_RET = W_CONV + 4 * RET_W


def _arrange_w_in(w_in):
    offs = np.cumsum([0, NSA_Q_W] + [NSA_KV_W] * 6 + [NSA_GATE_W, 2 * CONV_CH] + [RET_W] * 4)
    q, kc, vc, ks, vs, kw, vw, gl, conv, rq, rk, rv, rg = [w_in[:, offs[i]:offs[i + 1]] for i in range(13)]
    scale = HEAD_DIM ** -0.5
    gpad = jnp.zeros((D_MODEL, LANES - NSA_GATE_W // 2), w_in.dtype)
    gates = jnp.concatenate([gl[:, :NSA_GATE_W // 2], gpad, gl[:, NSA_GATE_W // 2:], gpad], axis=1)
    w = jnp.concatenate([q * (scale * LOG2E), kw, kc, vc, ks, vs, vw, gates, conv, rq, rk * scale, rv, rg], axis=1)
    return w.astype(BF16)


def _norm_proj_body(x_ref, g_ref, w_ref, hm_ref, ck_ref, wd_ref, gates_ref, conv_ref, ret_ref, *, seq):
    x = x_ref[...]
    ms = jnp.mean(x * x, axis=-1, keepdims=True)
    h = (x * lax.rsqrt(ms + EPS) * g_ref[...]).astype(BF16)
    res = _dot(h, w_ref[:, 0:W_HM])
    for j in range(HM_SLABS + CK_SLABS):
        dst = hm_ref.at[j] if j < HM_SLABS else ck_ref.at[j - HM_SLABS]
        dst[...] = res[:, j * HEAD_DIM:(j + 1) * HEAD_DIM].astype(BF16)
    rows = x.shape[0]
    t0 = (pl.program_id(0) % (seq // rows)) * rows
    lane = lax.broadcasted_iota(jnp.int32, (rows, HEAD_DIM), 1)
    blk = (t0 + lax.broadcasted_iota(jnp.int32, (rows, HEAD_DIM), 0)) // SLC_BLOCK
    onehot = jnp.where(blk == lane, 1.0, 0.0)
    ones_col = jnp.where(lane == 0, 1.0, 0.0)
    wide = _dot(h, w_ref[:, W_HM:W_WD])
    for j, ext in ((WD_KS, onehot), (WD_VS, ones_col), (WD_VW, ones_col)):
        for g in range(NSA_KV_HEADS):
            c = (j + g) * HEAD_DIM
            wd_ref[j + g] = jnp.concatenate([wide[:, c:c + HEAD_DIM], ext], axis=1).astype(BF16)
    gates_ref[...] = _dot(h, w_ref[:, W_WD:W_GATE])
    conv_ref[...] = _dot(h, w_ref[:, W_GATE:W_CONV]).astype(BF16)
    ret_ref[...] = _dot(h, w_ref[:, W_CONV:W_RET]).astype(BF16)


def _norm_proj(x2, ln_g, w_arr, seq):
    n = x2.shape[0]
    tm = ROW_TILE
    return pl.pallas_call(
        functools.partial(_norm_proj_body, seq=seq),
        grid=(n // tm,),
        in_specs=[
            pl.BlockSpec((tm, D_MODEL), lambda i: (i, 0)),
            pl.BlockSpec((1, D_MODEL), lambda i: (0, 0)),
            pl.BlockSpec((D_MODEL, W_RET), lambda i: (0, 0)),
        ],
        out_specs=[
            pl.BlockSpec((HM_SLABS, tm, HEAD_DIM), lambda i: (0, i, 0)),
            pl.BlockSpec((CK_SLABS, tm, HEAD_DIM), lambda i: (0, i, 0)),
            pl.BlockSpec((WD_SLABS, tm, LANES), lambda i: (0, i, 0)),
            pl.BlockSpec((tm, 2 * LANES), lambda i: (i, 0)),
            pl.BlockSpec((tm, 2 * CONV_CH), lambda i: (i, 0)),
            pl.BlockSpec((tm, 4 * RET_W), lambda i: (i, 0)),
        ],
        out_shape=[
            jax.ShapeDtypeStruct((HM_SLABS, n, HEAD_DIM), BF16),
            jax.ShapeDtypeStruct((CK_SLABS, n, HEAD_DIM), BF16),
            jax.ShapeDtypeStruct((WD_SLABS, n, LANES), BF16),
            jax.ShapeDtypeStruct((n, 2 * LANES), F32),
            jax.ShapeDtypeStruct((n, 2 * CONV_CH), BF16),
            jax.ShapeDtypeStruct((n, 4 * RET_W), BF16),
        ],
        compiler_params=_params("parallel"),
        name="norm_proj",
    )(x2, ln_g.reshape(1, D_MODEL), w_arr)


def _compress_body(s_ref, pe_ref, w1_ref, w2_ref, o_ref):
    u = _dot(s_ref[...], w1_ref[...])
    pe = pe_ref[...]
    bias = (_dot(pe[:, :CMP_STRIDE * HEAD_DIM], w1_ref[:, :CMP_HIDDEN])
            + _dot(pe[:, CMP_STRIDE * HEAD_DIM:], w1_ref[:, CMP_HIDDEN:]))[0:1]
    bot = u[:, CMP_HIDDEN:]
    bot_next = jnp.concatenate([bot[1:], bot[:1]], axis=0)
    hid = u[:, :CMP_HIDDEN] + bot_next + bias
    hid = hid * jax.nn.sigmoid(hid)
    o_ref[...] = _dot(hid.astype(BF16), w2_ref[...]).astype(BF16)


def _compress(sub, pe8, w1cat, w2):
    _, b, nsub, width = sub.shape
    return pl.pallas_call(
        _compress_body,
        grid=(4, b),
        in_specs=[
            pl.BlockSpec((None, None, nsub, width), lambda j, i: (j, i, 0, 0)),
            pl.BlockSpec((None, 8, 2 * width), lambda j, i: (j // 2, 0, 0)),
            pl.BlockSpec((None, width, 2 * CMP_HIDDEN), lambda j, i: (j // 2, 0, 0)),
            pl.BlockSpec((None, CMP_HIDDEN, HEAD_DIM), lambda j, i: (j // 2, 0, 0)),
        ],
        out_specs=pl.BlockSpec((None, None, nsub, HEAD_DIM), lambda j, i: (j, i, 0, 0)),
        out_shape=jax.ShapeDtypeStruct((4, b, nsub, HEAD_DIM), BF16),
        compiler_params=_params("parallel", "parallel"),
        name="compress",
    )(sub, pe8, w1cat, w2)


def _topk_block_bias(imp_t):
    ns, nq = imp_t.shape
    sub = 8
    nb = ns // sub
    blocks = [imp_t[v * sub:(v + 1) * sub] for v in range(nb)]
    cnts = [jnp.zeros((sub, nq), jnp.int32) for _ in range(nb)]
    sub_iota = lax.broadcasted_iota(jnp.int32, (sub, nq), 0)
    for i in range(ns):
        row = jnp.broadcast_to(imp_t[i:i + 1], (sub, nq))
        for v in range(nb):
            if v > i // sub:
                cnts[v] = cnts[v] + jnp.where(row >= blocks[v], 1, 0)
            elif v < i // sub:
                cnts[v] = cnts[v] + jnp.where(row > blocks[v], 1, 0)
            else:
                tie = (row == blocks[v]) & (sub_iota > i % sub)
                cnts[v] = cnts[v] + jnp.where(row > blocks[v], 1, 0) + jnp.where(tie, 1, 0)
    cnt = jnp.concatenate(cnts, axis=0)
    return jnp.where(cnt < SLC_TOPK, 0.0, NEG_INF)


def _flash_update(qs, k, v1, bias, state):
    nh = len(state)
    qn = qs.shape[0] // nh
    s_all = _dot_nt(qs, k)
    ps, ms, alphas = [], [], []
    for h in range(nh):
        s = s_all[h * qn:(h + 1) * qn]
        if bias is not None:
            s = s + bias
        m = state[h][0]
        m_new = jnp.maximum(m, jnp.max(s, axis=-1, keepdims=True))
        ps.append(jnp.exp2(s - m_new).astype(BF16))
        alphas.append(jnp.exp2(m - m_new))
        ms.append(m_new)
    pv = _dot(jnp.concatenate(ps, axis=0), v1)
    return tuple((ms[h], alphas[h] * state[h][1] + pv[h * qn:(h + 1) * qn]) for h in range(nh))


def _flash_update_staged(qs, k, v1, bias, nh, s_scr, p_scr, m_scr, al_scr, acc_scr):
    rows, width = qs.shape[0], k.shape[0]
    qn = rows // nh
    s_scr[...] = _dot_nt(qs, k)
    for b in range(rows // ROW_BLK):
        r = slice(b * ROW_BLK, (b + 1) * ROW_BLK)
        q0 = (b * ROW_BLK) % qn
        s = s_scr[r] + bias[q0:q0 + ROW_BLK]
        m_old = m_scr[r]
        m_new = jnp.maximum(m_old, jnp.broadcast_to(jnp.max(s, axis=-1, keepdims=True), (ROW_BLK, LANES)))
        al_scr[r] = jnp.exp2(m_old - m_new)
        m_scr[r] = m_new
        p_scr[r] = jnp.exp2(s - pltpu.repeat(m_new, width // LANES, axis=1)).astype(BF16)
    acc_scr[...] = al_scr[...] * acc_scr[...] + _dot(p_scr[...], v1)


def _nsa_body(q_ref, gl_ref, kc_ref, vc_ref, ks_ref, vs_ref, kw_ref, vw_ref, ovt_ref, o_ref,
              s_scr, p_scr, m_sel, al_sel, acc_sel):
    i = pl.program_id(2)
    hpg = NSA_GROUP
    q = q_ref[...].reshape(hpg * Q_TILE, HEAD_DIM)
    rr = lax.broadcasted_iota(jnp.int32, (Q_TILE, 1), 0)
    t_q = i * Q_TILE + rr
    heads = [slice(h * Q_TILE, (h + 1) * Q_TILE) for h in range(hpg)]

    kc = kc_ref[...]
    ncb = kc.shape[0]
    s_c = _dot_nt(q, kc)
    cmask = lax.broadcasted_iota(jnp.int32, (1, ncb), 1) * CMP_STRIDE + (CMP_LEN - 1) <= t_q
    pcs, psum = [], None
    for r in heads:
        s = jnp.where(cmask, s_c[r], NEG_INF)
        p = jnp.where(cmask, jnp.exp2(s - jnp.max(s, axis=-1, keepdims=True)), 0.0)
        l = jnp.sum(p, axis=-1, keepdims=True)
        p = p * (1.0 / jnp.where(l > 0.0, l, 1.0))
        pcs.append(p.astype(BF16))
        psum = p if psum is None else psum + p
    o_cmp = _dot(jnp.concatenate(pcs, axis=0), vc_ref[...])

    ovt = ovt_ref[...]
    imp_t = sum(_dot_nt(ovt, part) for part in _split3(psum))
    ns = imp_t.shape[0]
    sblk = lax.broadcasted_iota(jnp.int32, (ns, Q_TILE), 0)
    cur = (i * Q_TILE + lax.broadcasted_iota(jnp.int32, (ns, Q_TILE), 1)) // SLC_BLOCK
    forced = (sblk == 0) | (sblk == cur) | (sblk == cur - 1)
    imp_t = jnp.where(sblk <= cur, jnp.where(forced, FORCE_SCORE, imp_t), NEG_INF)
    bias_t = _topk_block_bias(imp_t)
    bias_pad = jnp.concatenate([bias_t, jnp.zeros((LANES - ns, Q_TILE), F32)], axis=0).T
    bias = bias_pad[:, :ns].astype(BF16)
    qp = jnp.concatenate([q, jnp.concatenate([bias] * hpg, axis=0)], axis=1)

    m_sel[...] = jnp.full(m_sel.shape, NEG_INF, F32)
    acc_sel[...] = jnp.zeros(acc_sel.shape, F32)
    n_full = (i * Q_TILE) // SEL_CHUNK

    def sel_step(c, carry):
        k0 = pl.multiple_of(c * SEL_CHUNK, SEL_CHUNK)
        kpos = k0 + lax.broadcasted_iota(jnp.int32, (1, SEL_CHUNK), 1)
        causal = jnp.where(kpos <= t_q, 0.0, NEG_INF)
        _flash_update_staged(qp, ks_ref[pl.ds(k0, SEL_CHUNK), :], vs_ref[pl.ds(k0, SEL_CHUNK), :], causal, hpg,
                             s_scr, p_scr, m_sel, al_sel, acc_sel)
        return carry

    lax.fori_loop(0, n_full + 1, sel_step, 0)

    wkeys = WINDOW + WIN_TILE
    init_w = tuple((jnp.full((WIN_TILE, 1), NEG_INF, F32), jnp.zeros((WIN_TILE, LANES), F32)) for _ in range(hpg))
    st_w = []
    for u in range(Q_TILE // WIN_TILE):
        sub = slice(u * WIN_TILE, (u + 1) * WIN_TILE)
        q_u = jnp.concatenate([q[r][sub] for r in heads], axis=0)
        w0 = pl.multiple_of(jnp.maximum(i * Q_TILE + u * WIN_TILE - WINDOW, 0), WIN_TILE)
        dpos = t_q[sub] - (w0 + lax.broadcasted_iota(jnp.int32, (1, wkeys), 1))
        band = jnp.where((dpos >= 0) & (dpos < WINDOW), 0.0, NEG_INF)
        st_w.append(_flash_update(q_u, kw_ref[pl.ds(w0, wkeys), :], vw_ref[pl.ds(w0, wkeys), :], band, init_w))

    sig = jax.nn.sigmoid(gl_ref[...])
    outs = []
    for h, r in enumerate(heads):
        c = h * N_BRANCH
        acc_s = acc_sel[r]
        acc_w = jnp.concatenate([st_u[h][1] for st_u in st_w], axis=0)
        g_s = sig[:, c + 1:c + 2] / acc_s[:, HEAD_DIM:HEAD_DIM + 1]
        g_w = sig[:, c + 2:c + 3] / acc_w[:, HEAD_DIM:HEAD_DIM + 1]
        outs.append(sig[:, c:c + 1] * o_cmp[r] + g_s * acc_s[:, :HEAD_DIM] + g_w * acc_w[:, :HEAD_DIM])
    o_ref[...] = jnp.concatenate(outs, axis=1).astype(BF16)


def _nsa(hm, wd, gates, kvc, ovt, batch, seq):
    n = batch * seq
    nq = seq // Q_TILE
    ncb = kvc.shape[2]
    g_ = NSA_KV_HEADS
    rows = NSA_GROUP * Q_TILE
    return pl.pallas_call(
        _nsa_body,
        grid=(batch, g_, nq),
        in_specs=[
            pl.BlockSpec((NSA_GROUP, Q_TILE, HEAD_DIM), lambda b, g, i: (g, b * nq + i, 0)),
            pl.BlockSpec((Q_TILE, LANES), lambda b, g, i: (b * nq + i, g)),
            pl.BlockSpec((None, None, ncb, HEAD_DIM), lambda b, g, i: (g, b, 0, 0)),
            pl.BlockSpec((None, None, ncb, HEAD_DIM), lambda b, g, i: (g_ + g, b, 0, 0)),
            pl.BlockSpec((None, seq, LANES), lambda b, g, i: (WD_KS + g, b, 0)),
            pl.BlockSpec((None, seq, LANES), lambda b, g, i: (WD_VS + g, b, 0)),
            pl.BlockSpec((None, seq, HEAD_DIM), lambda b, g, i: (HM_KW + g, b, 0)),
            pl.BlockSpec((None, seq, LANES), lambda b, g, i: (WD_VW + g, b, 0)),
            pl.BlockSpec(ovt.shape, lambda b, g, i: (0, 0)),
        ],
        out_specs=pl.BlockSpec((Q_TILE, NSA_GROUP * HEAD_DIM), lambda b, g, i: (b * nq + i, g)),
        out_shape=jax.ShapeDtypeStruct((n, NSA_Q_W), BF16),
        scratch_shapes=[
            pltpu.VMEM((rows, SEL_CHUNK), F32), pltpu.VMEM((rows, SEL_CHUNK), BF16),
            pltpu.VMEM((rows, LANES), F32), pltpu.VMEM((rows, LANES), F32), pltpu.VMEM((rows, LANES), F32),
        ],
        compiler_params=_params("parallel", "parallel", "arbitrary"),
        name="nsa",
    )(hm, gates, kvc, kvc, wd, wd, hm, wd, ovt)


def _conv_body(u_ref, up_ref, w_ref, b_ref, lg_ref, lb_ref, pw_ref, o_ref, hbuf, sbuf):
    i = pl.program_id(1)

    def glu(u):
        u = u.astype(F32)
        return u[:, :CONV_CH] * jax.nn.sigmoid(u[:, CONV_CH:])

    prev = glu(up_ref[...])
    hbuf[0:CONV_HALO] = jnp.where(i > 0, prev, 0.0)
    hbuf[CONV_HALO:] = glu(u_ref[...])
    span = sbuf.shape[1]
    for s in range(1, SUBLANES):
        sbuf[s - 1] = hbuf[s:s + span]
    w = w_ref[...]
    acc = jnp.zeros((CONV_TILE, CONV_CH), F32) + b_ref[...]
    base = CONV_HALO - (CONV_K - 1)
    for j in range(CONV_K):
        s = (base + j) % SUBLANES
        a8 = base + j - s
        src = hbuf[a8:a8 + CONV_TILE] if s == 0 else sbuf[s - 1, a8:a8 + CONV_TILE]
        acc = acc + w[j:j + 1] * src
    mu = jnp.mean(acc, axis=-1, keepdims=True)
    var = jnp.mean(jnp.square(acc - mu), axis=-1, keepdims=True)
    hf = (acc - mu) * lax.rsqrt(var + 1e-5) * lg_ref[...] + lb_ref[...]
    hf = hf * jax.nn.sigmoid(hf)
    o_ref[...] = _dot(hf.astype(BF16), pw_ref[...]).astype(BF16)


def _conv(conv_in, w32, b, lg, lb, pw, batch, seq):
    n = batch * seq
    nt = seq // CONV_TILE
    halo_per_tile = CONV_TILE // CONV_HALO
    row = lambda v: v.reshape(1, CONV_CH)
    return pl.pallas_call(
        _conv_body,
        grid=(batch, nt),
        in_specs=[
            pl.BlockSpec((CONV_TILE, 2 * CONV_CH), lambda b_, i: (b_ * nt + i, 0)),
            pl.BlockSpec((CONV_HALO, 2 * CONV_CH),
                         lambda b_, i: (jnp.maximum((b_ * nt + i) * halo_per_tile - 1, 0), 0)),
            pl.BlockSpec((32, CONV_CH), lambda b_, i: (0, 0)),
            pl.BlockSpec((1, CONV_CH), lambda b_, i: (0, 0)),
            pl.BlockSpec((1, CONV_CH), lambda b_, i: (0, 0)),
            pl.BlockSpec((1, CONV_CH), lambda b_, i: (0, 0)),
            pl.BlockSpec((CONV_CH, CONV_CH), lambda b_, i: (0, 0)),
        ],
        out_specs=pl.BlockSpec((CONV_TILE, CONV_CH), lambda b_, i: (b_ * nt + i, 0)),
        out_shape=jax.ShapeDtypeStruct((n, CONV_CH), BF16),
        scratch_shapes=[pltpu.VMEM((CONV_HALO + CONV_TILE, CONV_CH), F32),
                        pltpu.VMEM((SUBLANES - 1, CONV_HALO + CONV_TILE - SUBLANES, CONV_CH), F32)],
        compiler_params=_params("parallel", "parallel"),
        name="conformer_conv",
    )(conv_in, conv_in, w32, row(b), row(lg), row(lb), pw)


def _ret_body(q_ref, k_ref, v_ref, g_ref, cos_ref, sin_ref, perm_ref, decay_ref, xi_ref, zeta_ref, maskk_ref,
              maskv_ref, gmat_ref, masks_ref, mavg_ref, gn_ref, o_ref, state):
    c = pl.program_id(1)

    @pl.when(c == 0)
    def _():
        state[...] = jnp.zeros_like(state)

    nb = q_ref.shape[0]
    chunk = q_ref.shape[1]
    cos, sin = cos_ref[...], sin_ref[...]
    qk = jnp.concatenate([q_ref[bi] for bi in range(nb)] + [k_ref[bi] for bi in range(nb)], axis=0)
    rot = _dot(qk, perm_ref[...])
    outs, new_states = [], []
    for bi in range(nb):
        qs, ks = slice(bi * chunk, (bi + 1) * chunk), slice((nb + bi) * chunk, (nb + bi + 1) * chunk)
        qb = (q_ref[bi].astype(F32) * cos + rot[qs] * sin).astype(BF16)
        kr_t = (k_ref[bi].astype(F32) * cos + rot[ks] * sin).T
        k_bd = jnp.concatenate([kr_t.astype(BF16)] * RET_HEADS, axis=1) * maskk_ref[...]
        a = _dot(qb, k_bd) * decay_ref[...]
        v = v_ref[bi]
        v_bd = jnp.concatenate([v] * RET_HEADS, axis=0) * maskv_ref[...]
        s_bd = state[bi]
        outs.append(_dot(a.astype(BF16), v_bd) + _dot(qb, s_bd.astype(BF16)) * xi_ref[...])
        kz_t = (kr_t * zeta_ref[...]).astype(BF16)
        new_states.append(gmat_ref[...] * s_bd + _dot(kz_t, v) * masks_ref[...])
    o = jnp.concatenate(outs, axis=0)

    def head_mean(x):
        hi = x.astype(BF16)
        lo = (x - hi.astype(F32)).astype(BF16)
        return _dot(hi, mavg_ref[...]) + _dot(lo, mavg_ref[...])

    d = o - head_mean(o)
    y = d * lax.rsqrt(head_mean(d * d) + 1e-5) * gn_ref[...]
    for bi in range(nb):
        gate = g_ref[bi].astype(F32)
        o_ref[bi] = (gate * jax.nn.sigmoid(gate) * y[bi * chunk:(bi + 1) * chunk]).astype(BF16)
        state[bi] = new_states[bi]


def _retention_tables(seq):
    half = HEAD_DIM // 2
    pos = jnp.arange(seq, dtype=F32)
    inv = ROPE_BASE ** (-jnp.arange(half, dtype=F32) / half)
    ang = pos[:, None] * inv[None, :]
    cos, sin = jnp.cos(ang), jnp.sin(ang)
    cos_t = jnp.tile(jnp.concatenate([cos, cos], axis=1), (1, RET_HEADS))
    sin_t = jnp.tile(jnp.concatenate([-sin, sin], axis=1), (1, RET_HEADS))
    log_gamma = jnp.log1p(-jnp.exp2(-5.0 - jnp.arange(RET_HEADS, dtype=F32)))
    i = jnp.arange(RET_CHUNK, dtype=F32)
    rel = i[:, None] - i[None, :]
    decay = jnp.where(rel >= 0, jnp.exp(jnp.maximum(rel, 0.0)[None] * log_gamma[:, None, None]), 0.0)
    decay_cat = decay.transpose(1, 0, 2).reshape(RET_CHUNK, RET_HEADS * RET_CHUNK)
    xi = jnp.exp((i + 1.0)[None, :] * log_gamma[:, None])
    zeta = jnp.exp((RET_CHUNK - 1.0 - i)[None, :] * log_gamma[:, None])
    g_chunk = jnp.exp(RET_CHUNK * log_gamma)
    lanes = lambda a: jnp.repeat(a.T, HEAD_DIM, axis=1)
    j = np.arange(RET_W)
    partner = (j // HEAD_DIM) * HEAD_DIM + (j % HEAD_DIM + half) % HEAD_DIM
    perm = np.zeros((RET_W, RET_W), np.float32)
    perm[partner, j] = 1.0
    head_of_feat = j // HEAD_DIM
    head_of_key = np.arange(RET_HEADS * RET_CHUNK) // RET_CHUNK
    maskk = (head_of_feat[:, None] == head_of_key[None, :]).astype(np.float32)
    masks = (head_of_feat[:, None] == head_of_feat[None, :]).astype(np.float32)
    gmat = jnp.asarray(masks) * jnp.repeat(g_chunk, HEAD_DIM)[:, None]
    return (cos_t, sin_t, jnp.asarray(perm, BF16), decay_cat, lanes(xi), lanes(zeta).T, jnp.asarray(maskk, BF16),
            jnp.asarray(maskk.T, BF16), gmat, jnp.asarray(masks), jnp.asarray(masks / HEAD_DIM, BF16))


def _retention(ret, gn_g, tables, batch, seq):
    nc = seq // RET_CHUNK
    rb = RET_BATCH
    cos_t, sin_t = tables[:2]
    consts = tables[2:]
    ret3 = ret.reshape(batch, seq, 4 * RET_W)
    tile = lambda col: pl.BlockSpec((rb, RET_CHUNK, RET_W), lambda b, c: (b, c, col))
    const = lambda a: pl.BlockSpec(a.shape, lambda b, c: (0,) * a.ndim)
    out = pl.pallas_call(
        _ret_body,
        grid=(batch // rb, nc),
        in_specs=[
            tile(0), tile(1), tile(2), tile(3),
            pl.BlockSpec((RET_CHUNK, RET_W), lambda b, c: (c, 0)),
            pl.BlockSpec((RET_CHUNK, RET_W), lambda b, c: (c, 0)),
            *[const(a) for a in consts],
            pl.BlockSpec((1, RET_W), lambda b, c: (0, 0)),
        ],
        out_specs=pl.BlockSpec((rb, RET_CHUNK, RET_W), lambda b, c: (b, c, 0)),
        out_shape=jax.ShapeDtypeStruct((batch, seq, RET_W), BF16),
        scratch_shapes=[pltpu.VMEM((rb, RET_W, RET_W), F32)],
        compiler_params=_params("parallel", "arbitrary"),
        name="retention",
    )(ret3, ret3, ret3, ret3, cos_t, sin_t, *consts, gn_g.reshape(1, RET_W))
    return out.reshape(batch * seq, RET_W)


def _out_proj_body(*refs, with_router):
    if with_router:
        oa_ref, ob_ref, oc_ref, x_ref, w_ref, g_ref, rcat_ref, xo_ref, h_ref, lg_ref = refs
    else:
        oa_ref, ob_ref, oc_ref, x_ref, w_ref, g_ref, xo_ref, h_ref = refs
    y = (_dot(oa_ref[...], w_ref[0:NSA_Q_W])
         + _dot(ob_ref[...], w_ref[NSA_Q_W:NSA_Q_W + CONV_CH])
         + _dot(oc_ref[...], w_ref[NSA_Q_W + CONV_CH:]))
    x = x_ref[...] + y
    xo_ref[...] = x
    ms = jnp.mean(x * x, axis=-1, keepdims=True)
    h = x * lax.rsqrt(ms + EPS) * g_ref[...]
    hi = h.astype(BF16)
    h_ref[...] = hi
    if with_router:
        lo = (h - hi.astype(F32)).astype(BF16)
        t = _dot(hi, rcat_ref[...])
        lg_ref[...] = t[:, :LANES] + t[:, LANES:] + _dot(lo, rcat_ref[:, :LANES])


def _out_proj(oa, ob, oc, x2, w_out, ln_g, router=None):
    n = x2.shape[0]
    tm = ROW_TILE
    with_router = router is not None
    row = lambda w: pl.BlockSpec((tm, w), lambda i: (i, 0))
    const = lambda shape: pl.BlockSpec(shape, lambda i: (0, 0))
    in_specs = [row(NSA_Q_W), row(CONV_CH), row(RET_W), row(D_MODEL), const((D_MODEL, D_MODEL)), const((1, D_MODEL))]
    args = [oa, ob, oc, x2, w_out.astype(BF16), ln_g.reshape(1, D_MODEL)]
    out_specs = [row(D_MODEL), row(D_MODEL)]
    out_shape = [jax.ShapeDtypeStruct((n, D_MODEL), F32), jax.ShapeDtypeStruct((n, D_MODEL), BF16)]
    if with_router:
        rpad = jnp.pad(router, ((0, 0), (0, LANES - N_EXPERTS)))
        rhi = rpad.astype(BF16)
        rlo = (rpad - rhi.astype(F32)).astype(BF16)
        in_specs += [const((D_MODEL, 2 * LANES))]
        args += [jnp.concatenate([rhi, rlo], axis=1)]
        out_specs.append(row(LANES))
        out_shape.append(jax.ShapeDtypeStruct((n, LANES), F32))
    return pl.pallas_call(
        functools.partial(_out_proj_body, with_router=with_router),
        grid=(n // tm,),
        in_specs=in_specs,
        out_specs=out_specs,
        out_shape=out_shape,
        compiler_params=_params("parallel"),
        name="out_proj_router" if with_router else "out_proj",
    )(*args)


def _ffn_body(*refs, with_residual):
    if with_residual:
        e_ref, nv_ref, x_ref, wa_ref, wb_ref, w2_ref, r_ref, o_ref, acc = refs
    else:
        e_ref, nv_ref, x_ref, wa_ref, wb_ref, w2_ref, o_ref, acc = refs
    del e_ref
    f = pl.program_id(1)
    live = pl.program_id(0) < nv_ref[0]

    @pl.when(f == 0)
    def _():
        acc[...] = jnp.zeros_like(acc)

    @pl.when(live)
    def _():
        x = x_ref[...]
        a = _dot(x, wa_ref[...])
        b = _dot(x, wb_ref[...])
        hmid = (a * jax.nn.sigmoid(a) * b).astype(BF16)
        acc[...] += _dot(hmid, w2_ref[...])

    @pl.when(f == pl.num_programs(1) - 1)
    def _():
        o_ref[...] = (acc[...] + r_ref[...] if with_residual else acc[...]).astype(o_ref.dtype)


def _ffn(xs, blk_e, n_live, w13, w2, rows, out_dtype, residual=None):
    p = xs.shape[0]
    nf = D_FF // FF_TILE
    with_residual = residual is not None
    last = lambda i, nv: jnp.minimum(i, nv[0] - 1)
    in_specs = [
        pl.BlockSpec((rows, D_MODEL), lambda i, f, e, nv: (last(i, nv), 0)),
        pl.BlockSpec((None, D_MODEL, FF_TILE), lambda i, f, e, nv: (e[last(i, nv)], 0, f)),
        pl.BlockSpec((None, D_MODEL, FF_TILE), lambda i, f, e, nv: (e[last(i, nv)], 0, nf + f)),
        pl.BlockSpec((None, FF_TILE, D_MODEL), lambda i, f, e, nv: (e[last(i, nv)], f, 0)),
    ]
    args = [xs, w13, w13, w2]
    if with_residual:
        in_specs.append(pl.BlockSpec((rows, D_MODEL), lambda i, f, e, nv: (i, 0)))
        args.append(residual)
    return pl.pallas_call(
        functools.partial(_ffn_body, with_residual=with_residual),
        grid_spec=pltpu.PrefetchScalarGridSpec(
            num_scalar_prefetch=2,
            grid=(p // rows, nf),
            in_specs=in_specs,
            out_specs=pl.BlockSpec((rows, D_MODEL), lambda i, f, e, nv: (i, 0)),
            scratch_shapes=[pltpu.VMEM((rows, D_MODEL), F32)],
        ),
        out_shape=jax.ShapeDtypeStruct((p, D_MODEL), out_dtype),
        compiler_params=_params("parallel", "arbitrary"),
        name="ffn_residual" if with_residual else "ffn_grouped",
    )(blk_e, n_live, *args)


def _final_body(x_ref, y0_ref, y1_ref, gt_ref, g_ref, o_ref):
    gt = gt_ref[...]
    x = x_ref[...] + gt[:, 0:1] * y0_ref[...].astype(F32) + gt[:, 1:2] * y1_ref[...].astype(F32)
    ms = jnp.mean(x * x, axis=-1, keepdims=True)
    o_ref[...] = x * lax.rsqrt(ms + EPS) * g_ref[...]


def _final(x2, y0, y1, gates_pad, ln_g):
    n = x2.shape[0]
    tm = ROW_TILE
    row = lambda w: pl.BlockSpec((tm, w), lambda i: (i, 0))
    return pl.pallas_call(
        _final_body,
        grid=(n // tm,),
        in_specs=[row(D_MODEL), row(D_MODEL), row(D_MODEL), row(LANES), pl.BlockSpec((1, D_MODEL), lambda i: (0, 0))],
        out_specs=row(D_MODEL),
        out_shape=jax.ShapeDtypeStruct((n, D_MODEL), F32),
        compiler_params=_params("parallel"),
        name="moe_combine_final_norm",
    )(x2, y0, y1, gates_pad, ln_g.reshape(1, D_MODEL))


def _overlap_t(seq):
    nc = seq // CMP_STRIDE
    ns = seq // SLC_BLOCK
    cs = np.arange(nc)[None, :] * CMP_STRIDE
    ss = np.arange(ns)[:, None] * SLC_BLOCK
    return jnp.asarray(((cs < ss + SLC_BLOCK) & (cs + CMP_LEN > ss)).astype(np.float32), BF16)


def _route(logits, rows):
    n = logits.shape[0]
    top_logit, top_e = lax.top_k(logits, TOP_K)
    gates = jax.nn.softmax(top_logit, axis=-1)
    e_flat = top_e.reshape(-1).astype(jnp.int32)
    onehot = (e_flat[:, None] == jnp.arange(N_EXPERTS, dtype=jnp.int32)[None, :]).astype(jnp.int32)
    csum = jnp.cumsum(onehot, axis=0)
    counts = csum[-1]
    padded = ((counts + rows - 1) // rows) * rows
    starts = jnp.cumsum(counts) - counts
    pends = jnp.cumsum(padded)
    pstarts = pends - padded
    pos = jnp.sum(onehot * (csum - 1 + pstarts[None, :]), axis=1).reshape(n, TOP_K)
    p = n * TOP_K + N_EXPERTS * rows
    nblk = p // rows
    blk_e = jnp.minimum(jnp.searchsorted(pends, jnp.arange(nblk) * rows, side='right'), N_EXPERTS - 1)
    blk_e = blk_e.astype(jnp.int32)
    order = jnp.argsort(e_flat, stable=True).astype(jnp.int32)
    slot_e = jnp.repeat(blk_e, rows)
    off = jnp.arange(p, dtype=jnp.int32) - pstarts[slot_e]
    live = (off < counts[slot_e]) & (jnp.arange(p) < pends[-1])
    src = jnp.clip(starts[slot_e] + off, 0, n * TOP_K - 1)
    buf_tok = jnp.where(live, order[src] // TOP_K, 0).astype(jnp.int32)
    n_live = (pends[-1:] // rows).astype(jnp.int32)
    return gates, buf_tok, pos, blk_e, n_live


def _token_mixer(x2, l, batch, seq, tables, ovt, ln_attn, w_in, cmp_pe, cmp_w1, cmp_w2, conv_w, conv_b, conv_ln_g,
                 conv_ln_b, conv_pw, ret_gn):
    hm, ck, wd, gates, conv_in, ret = _norm_proj(x2, ln_attn[l], _arrange_w_in(w_in[l]), seq)
    nsub = seq // CMP_STRIDE
    half = CMP_STRIDE * HEAD_DIM
    sub = ck.reshape(CK_SLABS, batch, nsub, half)
    w1 = cmp_w1[l]
    w1cat = jnp.concatenate([w1[:, :half], w1[:, half:]], axis=2).astype(BF16)
    pe8 = jnp.broadcast_to(cmp_pe[l].reshape(2, 1, CMP_LEN * HEAD_DIM), (2, 8, CMP_LEN * HEAD_DIM)).astype(BF16)
    kvc = _compress(sub, pe8, w1cat, cmp_w2[l].astype(BF16))
    o_a = _nsa(hm, wd, gates, kvc, ovt, batch, seq)
    w32 = jnp.pad(conv_w[l], ((0, 32 - CONV_K), (0, 0)))
    o_b = _conv(conv_in, w32, conv_b[l], conv_ln_g[l], conv_ln_b[l], conv_pw[l].astype(BF16), batch, seq)
    o_c = _retention(ret, ret_gn[l], tables, batch, seq)
    return o_a, o_b, o_c


def kernel(x, ln_attn, w_in, cmp_pe, cmp_w1, cmp_w2, conv_w, conv_b, conv_ln_g, conv_ln_b, conv_pw, ret_gn, w_out,
           ln_ffn, ffn_w13, ffn_w2, router, moe_w13, moe_w2, ln_final):
    batch, seq, d = x.shape
    n = batch * seq
    depth = ln_attn.shape[0]
    x2 = x.reshape(n, d)
    tables = _retention_tables(seq)
    ovt = _overlap_t(seq)
    out = None
    for l in range(depth):
        o_a, o_b, o_c = _token_mixer(x2, l, batch, seq, tables, ovt, ln_attn, w_in, cmp_pe, cmp_w1, cmp_w2, conv_w,
                                     conv_b, conv_ln_g, conv_ln_b, conv_pw, ret_gn)
        if l % 2 == 0:
            x2, h2 = _out_proj(o_a, o_b, o_c, x2, w_out[l], ln_ffn[l])
            nblk = n // FFN_ROWS
            x2 = _ffn(h2, jnp.zeros((nblk,), jnp.int32), jnp.full((1,), nblk, jnp.int32),
                      ffn_w13[l // 2][None].astype(BF16), ffn_w2[l // 2][None].astype(BF16), FFN_ROWS, F32, residual=x2)
        else:
            x2, h2, logits = _out_proj(o_a, o_b, o_c, x2, w_out[l], ln_ffn[l], router=router[l // 2])
            gates, buf_tok, pos, blk_e, n_live = _route(logits[:, :N_EXPERTS], MOE_ROWS)
            xs = h2[buf_tok]
            yb = _ffn(xs, blk_e, n_live, moe_w13[l // 2].astype(BF16), moe_w2[l // 2].astype(BF16), MOE_ROWS, BF16)
            gates_pad = jnp.pad(gates, ((0, 0), (0, LANES - TOP_K)))
            if l == depth - 1:
                out = _final(x2, yb[pos[:, 0]], yb[pos[:, 1]], gates_pad, ln_final)
            else:
                x2 = x2 + gates[:, 0:1] * yb[pos[:, 0]] + gates[:, 1:2] * yb[pos[:, 1]]
    if out is None:
        out = _final(x2, jnp.zeros_like(x2), jnp.zeros_like(x2), jnp.zeros((n, LANES), F32), ln_final)
    return out.reshape(batch, seq, d)
```

```python
import functools

import numpy as np
import jax
import jax.numpy as jnp
from jax import lax
from jax.experimental import pallas as pl
from jax.experimental.pallas import tpu as pltpu

D_MODEL = 1024
HEAD_DIM = 64
NSA_HEADS = 8
NSA_KV_HEADS = 2
NSA_GROUP = NSA_HEADS // NSA_KV_HEADS
CMP_LEN = 32
CMP_STRIDE = 16
CMP_HIDDEN = 256
SLC_BLOCK = 64
SLC_TOPK = 16
WINDOW = 512
N_BRANCH = 3
CONV_CH = 256
CONV_K = 31
RET_HEADS = 4
RET_CHUNK = 128
ROPE_BASE = 10000.0
D_FF = 2816
N_EXPERTS = 8
TOP_K = 2
NEG_INF = -1e30
FORCE_SCORE = 1e9
EPS = 1e-6

NSA_Q_W = NSA_HEADS * HEAD_DIM
NSA_KV_W = NSA_KV_HEADS * HEAD_DIM
NSA_GATE_W = NSA_HEADS * N_BRANCH
RET_W = RET_HEADS * HEAD_DIM

LANES = 128
SUBLANES = 8
Q_TILE = 512
WIN_TILE = 256
ROW_BLK = 32
SEL_CHUNK = 512
ROW_TILE = 512
RET_BATCH = 8
CONV_TILE = 512
CONV_HALO = 32
MOE_ROWS = 512
FFN_ROWS = 512
FF_TILE = 1408
VMEM_LIMIT = 56 * 1024 * 1024

BF16 = jnp.bfloat16
F32 = jnp.float32

HM_Q, HM_KW, HM_SLABS = 0, 8, 10
CK_SLABS = 4
WD_KS, WD_VS, WD_VW, WD_SLABS = 0, 2, 4, 6
LOG2E = 1.4426950408889634


def _dot(a, b):
    return jnp.dot(a, b, preferred_element_type=F32)


def _dot_nt(a, b):
    return lax.dot_general(a, b, (((1,), (1,)), ((), ())), preferred_element_type=F32)


def _split3(x):
    hi = x.astype(BF16)
    r1 = x - hi.astype(F32)
    mid = r1.astype(BF16)
    lo = (r1 - mid.astype(F32)).astype(BF16)
    return hi, mid, lo


def _params(*sem):
    return pltpu.CompilerParams(dimension_semantics=sem, vmem_limit_bytes=VMEM_LIMIT)


W_HM = (HM_SLABS + CK_SLABS) * HEAD_DIM
W_WD = W_HM + 3 * NSA_KV_W
W_GATE = W_WD + 2 * LANES
W_CONV = W_GATE + 2 * CONV_CH
W_RET = W_CONV + 4 * RET_W


def _arrange_w_in(w_in):
    offs = np.cumsum([0, NSA_Q_W] + [NSA_KV_W] * 6 + [NSA_GATE_W, 2 * CONV_CH] + [RET_W] * 4)
    q, kc, vc, ks, vs, kw, vw, gl, conv, rq, rk, rv, rg = [w_in[:, offs[i]:offs[i + 1]] for i in range(13)]
    scale = HEAD_DIM ** -0.5
    gpad = jnp.zeros((D_MODEL, LANES - NSA_GATE_W // 2), w_in.dtype)
    gates = jnp.concatenate([gl[:, :NSA_GATE_W // 2], gpad, gl[:, NSA_GATE_W // 2:], gpad], axis=1)
    w = jnp.concatenate([q * (scale * LOG2E), kw, kc, vc, ks, vs, vw, gates, conv, rq, rk * scale, rv, rg], axis=1)
    return w.astype(BF16)


def _norm_proj_body(x_ref, g_ref, w_ref, hm_ref, ck_ref, wd_ref, gates_ref, conv_ref, ret_ref, *, seq):
    x = x_ref[...]
    ms = jnp.mean(x * x, axis=-1, keepdims=True)
    h = (x * lax.rsqrt(ms + EPS) * g_ref[...]).astype(BF16)
    res = _dot(h, w_ref[:, 0:W_HM])
    for j in range(HM_SLABS + CK_SLABS):
        dst = hm_ref.at[j] if j < HM_SLABS else ck_ref.at[j - HM_SLABS]
        dst[...] = res[:, j * HEAD_DIM:(j + 1) * HEAD_DIM].astype(BF16)
    rows = x.shape[0]
    t0 = (pl.program_id(0) % (seq // rows)) * rows
    lane = lax.broadcasted_iota(jnp.int32, (rows, HEAD_DIM), 1)
    blk = (t0 + lax.broadcasted_iota(jnp.int32, (rows, HEAD_DIM), 0)) // SLC_BLOCK
    onehot = jnp.where(blk == lane, 1.0, 0.0)
    ones_col = jnp.where(lane == 0, 1.0, 0.0)
    wide = _dot(h, w_ref[:, W_HM:W_WD])
    for j, ext in ((WD_KS, onehot), (WD_VS, ones_col), (WD_VW, ones_col)):
        for g in range(NSA_KV_HEADS):
            c = (j + g) * HEAD_DIM
            wd_ref[j + g] = jnp.concatenate([wide[:, c:c + HEAD_DIM], ext], axis=1).astype(BF16)
    gates_ref[...] = _dot(h, w_ref[:, W_WD:W_GATE])
    conv_ref[...] = _dot(h, w_ref[:, W_GATE:W_CONV]).astype(BF16)
    ret_ref[...] = _dot(h, w_ref[:, W_CONV:W_RET]).astype(BF16)


def _norm_proj(x2, ln_g, w_arr, seq):
    n = x2.shape[0]
    tm = ROW_TILE
    return pl.pallas_call(
        functools.partial(_norm_proj_body, seq=seq),
        grid=(n // tm,),
        in_specs=[
            pl.BlockSpec((tm, D_MODEL), lambda i: (i, 0)),
            pl.BlockSpec((1, D_MODEL), lambda i: (0, 0)),
            pl.BlockSpec((D_MODEL, W_RET), lambda i: (0, 0)),
        ],
        out_specs=[
            pl.BlockSpec((HM_SLABS, tm, HEAD_DIM), lambda i: (0, i, 0)),
            pl.BlockSpec((CK_SLABS, tm, HEAD_DIM), lambda i: (0, i, 0)),
            pl.BlockSpec((WD_SLABS, tm, LANES), lambda i: (0, i, 0)),
            pl.BlockSpec((tm, 2 * LANES), lambda i: (i, 0)),
            pl.BlockSpec((tm, 2 * CONV_CH), lambda i: (i, 0)),
            pl.BlockSpec((tm, 4 * RET_W), lambda i: (i, 0)),
        ],
        out_shape=[
            jax.ShapeDtypeStruct((HM_SLABS, n, HEAD_DIM), BF16),
            jax.ShapeDtypeStruct((CK_SLABS, n, HEAD_DIM), BF16),
            jax.ShapeDtypeStruct((WD_SLABS, n, LANES), BF16),
            jax.ShapeDtypeStruct((n, 2 * LANES), F32),
            jax.ShapeDtypeStruct((n, 2 * CONV_CH), BF16),
            jax.ShapeDtypeStruct((n, 4 * RET_W), BF16),
        ],
        compiler_params=_params("parallel"),
        name="norm_proj",
    )(x2, ln_g.reshape(1, D_MODEL), w_arr)


def _compress_body(s_ref, pe_ref, w1_ref, w2_ref, o_ref):
    u = _dot(s_ref[...], w1_ref[...])
    pe = pe_ref[...]
    bias = (_dot(pe[:, :CMP_STRIDE * HEAD_DIM], w1_ref[:, :CMP_HIDDEN])
            + _dot(pe[:, CMP_STRIDE * HEAD_DIM:], w1_ref[:, CMP_HIDDEN:]))[0:1]
    bot = u[:, CMP_HIDDEN:]
    bot_next = jnp.concatenate([bot[1:], bot[:1]], axis=0)
    hid = u[:, :CMP_HIDDEN] + bot_next + bias
    hid = hid * jax.nn.sigmoid(hid)
    o_ref[...] = _dot(hid.astype(BF16), w2_ref[...]).astype(BF16)


def _compress(sub, pe8, w1cat, w2):
    _, b, nsub, width = sub.shape
    return pl.pallas_call(
        _compress_body,
        grid=(4, b),
        in_specs=[
            pl.BlockSpec((None, None, nsub, width), lambda j, i: (j, i, 0, 0)),
            pl.BlockSpec((None, 8, 2 * width), lambda j, i: (j // 2, 0, 0)),
            pl.BlockSpec((None, width, 2 * CMP_HIDDEN), lambda j, i: (j // 2, 0, 0)),
            pl.BlockSpec((None, CMP_HIDDEN, HEAD_DIM), lambda j, i: (j // 2, 0, 0)),
        ],
        out_specs=pl.BlockSpec((None, None, nsub, HEAD_DIM), lambda j, i: (j, i, 0, 0)),
        out_shape=jax.ShapeDtypeStruct((4, b, nsub, HEAD_DIM), BF16),
        compiler_params=_params("parallel", "parallel"),
        name="compress",
    )(sub, pe8, w1cat, w2)


def _topk_block_bias(imp_t):
    ns, nq = imp_t.shape
    sub = 8
    nb = ns // sub
    blocks = [imp_t[v * sub:(v + 1) * sub] for v in range(nb)]
    cnts = [jnp.zeros((sub, nq), jnp.int32) for _ in range(nb)]
    sub_iota = lax.broadcasted_iota(jnp.int32, (sub, nq), 0)
    for i in range(ns):
        row = jnp.broadcast_to(imp_t[i:i + 1], (sub, nq))
        for v in range(nb):
            if v > i // sub:
                cnts[v] = cnts[v] + jnp.where(row >= blocks[v], 1, 0)
            elif v < i // sub:
                cnts[v] = cnts[v] + jnp.where(row > blocks[v], 1, 0)
            else:
                tie = (row == blocks[v]) & (sub_iota > i % sub)
                cnts[v] = cnts[v] + jnp.where(row > blocks[v], 1, 0) + jnp.where(tie, 1, 0)
    cnt = jnp.concatenate(cnts, axis=0)
    return jnp.where(cnt < SLC_TOPK, 0.0, NEG_INF)


def _flash_update(qs, k, v1, bias, state):
    nh = len(state)
    qn = qs.shape[0] // nh
    s_all = _dot_nt(qs, k)
    ps, ms, alphas = [], [], []
    for h in range(nh):
        s = s_all[h * qn:(h + 1) * qn]
        if bias is not None:
            s = s + bias
        m = state[h][0]
        m_new = jnp.maximum(m, jnp.max(s, axis=-1, keepdims=True))
        ps.append(jnp.exp2(s - m_new).astype(BF16))
        alphas.append(jnp.exp2(m - m_new))
        ms.append(m_new)
    pv = _dot(jnp.concatenate(ps, axis=0), v1)
    return tuple((ms[h], alphas[h] * state[h][1] + pv[h * qn:(h + 1) * qn]) for h in range(nh))


def _flash_update_staged(qs, k, v1, bias, nh, s_scr, p_scr, m_scr, al_scr, acc_scr):
    rows, width = qs.shape[0], k.shape[0]
    qn = rows // nh
    s_scr[...] = _dot_nt(qs, k)
    for b in range(rows // ROW_BLK):
        r = slice(b * ROW_BLK, (b + 1) * ROW_BLK)
        q0 = (b * ROW_BLK) % qn
        s = s_scr[r] + bias[q0:q0 + ROW_BLK]
        m_old = m_scr[r]
        m_new = jnp.maximum(m_old, jnp.broadcast_to(jnp.max(s, axis=-1, keepdims=True), (ROW_BLK, LANES)))
        al_scr[r] = jnp.exp2(m_old - m_new)
        m_scr[r] = m_new
        p_scr[r] = jnp.exp2(s - jnp.concatenate([m_new] * (width // LANES), axis=1)).astype(BF16)
    acc_scr[...] = al_scr[...] * acc_scr[...] + _dot(p_scr[...], v1)


def _nsa_body(q_ref, gl_ref, kc_ref, vc_ref, ks_ref, vs_ref, kw_ref, vw_ref, ovt_ref, o_ref,
              s_scr, p_scr, m_sel, al_sel, acc_sel):
    i = pl.program_id(2)
    hpg = NSA_GROUP
    q = q_ref[...].reshape(hpg * Q_TILE, HEAD_DIM)
    rr = lax.broadcasted_iota(jnp.int32, (Q_TILE, 1), 0)
    t_q = i * Q_TILE + rr
    heads = [slice(h * Q_TILE, (h + 1) * Q_TILE) for h in range(hpg)]

    kc = kc_ref[...]
    ncb = kc.shape[0]
    s_c = _dot_nt(q, kc)
    c_end = lax.broadcasted_iota(jnp.int32, (1, ncb), 1) * CMP_STRIDE + (CMP_LEN - 1)
    cbias = jnp.where(c_end <= t_q, 0.0, NEG_INF)
    some_block = t_q >= CMP_LEN - 1
    pcs, psum = [], None
    for r in heads:
        s = s_c[r] + cbias
        p = jnp.exp2(s - jnp.max(s, axis=-1, keepdims=True))
        l = jnp.sum(p, axis=-1, keepdims=True)
        p = p * jnp.where(some_block, 1.0 / l, 0.0)
        pcs.append(p.astype(BF16))
        psum = p if psum is None else psum + p
    o_cmp = _dot(jnp.concatenate(pcs, axis=0), vc_ref[...])

    ovt = ovt_ref[...]
    imp_t = sum(_dot_nt(ovt, part) for part in _split3(psum))
    ns = imp_t.shape[0]
    sblk = lax.broadcasted_iota(jnp.int32, (ns, Q_TILE), 0)
    cur = (i * Q_TILE + lax.broadcasted_iota(jnp.int32, (ns, Q_TILE), 1)) // SLC_BLOCK
    forced = (sblk == 0) | (sblk == cur) | (sblk == cur - 1)
    imp_t = jnp.where(sblk <= cur, jnp.where(forced, FORCE_SCORE, imp_t), NEG_INF)
    few_blocks = (i + 1) * Q_TILE <= SLC_TOPK * SLC_BLOCK
    bias_t = lax.cond(few_blocks, lambda x: jnp.where(x > 0.5 * NEG_INF, 0.0, NEG_INF), _topk_block_bias, imp_t)
    bias_pad = jnp.concatenate([bias_t, jnp.zeros((LANES - ns, Q_TILE), F32)], axis=0).T
    bias = bias_pad[:, :ns].astype(BF16)
    qp = jnp.concatenate([q, jnp.concatenate([bias] * hpg, axis=0)], axis=1)

    m_sel[...] = jnp.full(m_sel.shape, NEG_INF, F32)
    acc_sel[...] = jnp.zeros(acc_sel.shape, F32)
    n_full = (i * Q_TILE) // SEL_CHUNK

    def sel_step(c, carry):
        k0 = pl.multiple_of(c * SEL_CHUNK, SEL_CHUNK)
        kpos = k0 + lax.broadcasted_iota(jnp.int32, (1, SEL_CHUNK), 1)
        causal = jnp.where(kpos <= t_q, 0.0, NEG_INF)
        _flash_update_staged(qp, ks_ref[pl.ds(k0, SEL_CHUNK), :], vs_ref[pl.ds(k0, SEL_CHUNK), :], causal, hpg,
                             s_scr, p_scr, m_sel, al_sel, acc_sel)
        return carry

    lax.fori_loop(0, n_full + 1, sel_step, 0)

    wkeys = WINDOW + WIN_TILE
    init_w = tuple((jnp.full((WIN_TILE, 1), NEG_INF, F32), jnp.zeros((WIN_TILE, LANES), F32)) for _ in range(hpg))
    st_w = []
    for u in range(Q_TILE // WIN_TILE):
        sub = slice(u * WIN_TILE, (u + 1) * WIN_TILE)
        q_u = jnp.concatenate([q[r][sub] for r in heads], axis=0)
        w0 = pl.multiple_of(jnp.maximum(i * Q_TILE + u * WIN_TILE - WINDOW, 0), WIN_TILE)
        dpos = t_q[sub] - (w0 + lax.broadcasted_iota(jnp.int32, (1, wkeys), 1))
        band = jnp.where((dpos >= 0) & (dpos < WINDOW), 0.0, NEG_INF)
        st_w.append(_flash_update(q_u, kw_ref[pl.ds(w0, wkeys), :], vw_ref[pl.ds(w0, wkeys), :], band, init_w))

    sig = jax.nn.sigmoid(gl_ref[...])
    outs = []
    for h, r in enumerate(heads):
        c = h * N_BRANCH
        acc_s = acc_sel[r]
        acc_w = jnp.concatenate([st_u[h][1] for st_u in st_w], axis=0)
        g_s = sig[:, c + 1:c + 2] / acc_s[:, HEAD_DIM:HEAD_DIM + 1]
        g_w = sig[:, c + 2:c + 3] / acc_w[:, HEAD_DIM:HEAD_DIM + 1]
        outs.append(sig[:, c:c + 1] * o_cmp[r] + g_s * acc_s[:, :HEAD_DIM] + g_w * acc_w[:, :HEAD_DIM])
    o_ref[...] = jnp.concatenate(outs, axis=1).astype(BF16)


def _nsa(hm, wd, gates, kvc, ovt, batch, seq):
    n = batch * seq
    nq = seq // Q_TILE
    ncb = kvc.shape[2]
    g_ = NSA_KV_HEADS
    rows = NSA_GROUP * Q_TILE
    return pl.pallas_call(
        _nsa_body,
        grid=(batch, g_, nq),
        in_specs=[
            pl.BlockSpec((NSA_GROUP, Q_TILE, HEAD_DIM), lambda b, g, i: (g, b * nq + i, 0)),
            pl.BlockSpec((Q_TILE, LANES), lambda b, g, i: (b * nq + i, g)),
            pl.BlockSpec((None, None, ncb, HEAD_DIM), lambda b, g, i: (g, b, 0, 0)),
            pl.BlockSpec((None, None, ncb, HEAD_DIM), lambda b, g, i: (g_ + g, b, 0, 0)),
            pl.BlockSpec((None, seq, LANES), lambda b, g, i: (WD_KS + g, b, 0)),
            pl.BlockSpec((None, seq, LANES), lambda b, g, i: (WD_VS + g, b, 0)),
            pl.BlockSpec((None, seq, HEAD_DIM), lambda b, g, i: (HM_KW + g, b, 0)),
            pl.BlockSpec((None, seq, LANES), lambda b, g, i: (WD_VW + g, b, 0)),
            pl.BlockSpec(ovt.shape, lambda b, g, i: (0, 0)),
        ],
        out_specs=pl.BlockSpec((Q_TILE, NSA_GROUP * HEAD_DIM), lambda b, g, i: (b * nq + i, g)),
        out_shape=jax.ShapeDtypeStruct((n, NSA_Q_W), BF16),
        scratch_shapes=[
            pltpu.VMEM((rows, SEL_CHUNK), F32), pltpu.VMEM((rows, SEL_CHUNK), BF16),
            pltpu.VMEM((rows, LANES), F32), pltpu.VMEM((rows, LANES), F32), pltpu.VMEM((rows, LANES), F32),
        ],
        compiler_params=_params("parallel", "parallel", "arbitrary"),
        name="nsa",
    )(hm, gates, kvc, kvc, wd, wd, hm, wd, ovt)


def _conv_body(u_ref, up_ref, w_ref, b_ref, lg_ref, lb_ref, pw_ref, o_ref, hbuf, sbuf):
    i = pl.program_id(1)

    def glu(u):
        u = u.astype(F32)
        return u[:, :CONV_CH] * jax.nn.sigmoid(u[:, CONV_CH:])

    prev = glu(up_ref[...])
    hbuf[0:CONV_HALO] = jnp.where(i > 0, prev, 0.0)
    hbuf[CONV_HALO:] = glu(u_ref[...])
    span = sbuf.shape[1]
    for s in range(1, SUBLANES):
        sbuf[s - 1] = hbuf[s:s + span]
    w = w_ref[...]
    acc = jnp.zeros((CONV_TILE, CONV_CH), F32) + b_ref[...]
    base = CONV_HALO - (CONV_K - 1)
    for j in range(CONV_K):
        s = (base + j) % SUBLANES
        a8 = base + j - s
        src = hbuf[a8:a8 + CONV_TILE] if s == 0 else sbuf[s - 1, a8:a8 + CONV_TILE]
        acc = acc + w[j:j + 1] * src
    mu = jnp.mean(acc, axis=-1, keepdims=True)
    var = jnp.mean(jnp.square(acc - mu), axis=-1, keepdims=True)
    hf = (acc - mu) * lax.rsqrt(var + 1e-5) * lg_ref[...] + lb_ref[...]
    hf = hf * jax.nn.sigmoid(hf)
    o_ref[...] = _dot(hf.astype(BF16), pw_ref[...]).astype(BF16)


def _conv(conv_in, w32, b, lg, lb, pw, batch, seq):
    n = batch * seq
    nt = seq // CONV_TILE
    halo_per_tile = CONV_TILE // CONV_HALO
    row = lambda v: v.reshape(1, CONV_CH)
    return pl.pallas_call(
        _conv_body,
        grid=(batch, nt),
        in_specs=[
            pl.BlockSpec((CONV_TILE, 2 * CONV_CH), lambda b_, i: (b_ * nt + i, 0)),
            pl.BlockSpec((CONV_HALO, 2 * CONV_CH),
                         lambda b_, i: (jnp.maximum((b_ * nt + i) * halo_per_tile - 1, 0), 0)),
            pl.BlockSpec((32, CONV_CH), lambda b_, i: (0, 0)),
            pl.BlockSpec((1, CONV_CH), lambda b_, i: (0, 0)),
            pl.BlockSpec((1, CONV_CH), lambda b_, i: (0, 0)),
            pl.BlockSpec((1, CONV_CH), lambda b_, i: (0, 0)),
            pl.BlockSpec((CONV_CH, CONV_CH), lambda b_, i: (0, 0)),
        ],
        out_specs=pl.BlockSpec((CONV_TILE, CONV_CH), lambda b_, i: (b_ * nt + i, 0)),
        out_shape=jax.ShapeDtypeStruct((n, CONV_CH), BF16),
        scratch_shapes=[pltpu.VMEM((CONV_HALO + CONV_TILE, CONV_CH), F32),
                        pltpu.VMEM((SUBLANES - 1, CONV_HALO + CONV_TILE - SUBLANES, CONV_CH), F32)],
        compiler_params=_params("parallel", "parallel"),
        name="conformer_conv",
    )(conv_in, conv_in, w32, row(b), row(lg), row(lb), pw)


def _ret_body(q_ref, k_ref, v_ref, g_ref, cos_ref, sin_ref, perm_ref, decay_ref, xi_ref, zeta_ref, maskk_ref,
              maskv_ref, gmat_ref, masks_ref, mavg_ref, gn_ref, o_ref, state):
    c = pl.program_id(1)

    @pl.when(c == 0)
    def _():
        state[...] = jnp.zeros_like(state)

    nb = q_ref.shape[0]
    chunk = q_ref.shape[1]
    cos, sin = cos_ref[...], sin_ref[...]
    qk = jnp.concatenate([q_ref[bi] for bi in range(nb)] + [k_ref[bi] for bi in range(nb)], axis=0)
    rot = _dot(qk, perm_ref[...])
    outs, new_states = [], []
    for bi in range(nb):
        qs, ks = slice(bi * chunk, (bi + 1) * chunk), slice((nb + bi) * chunk, (nb + bi + 1) * chunk)
        qb = (q_ref[bi].astype(F32) * cos + rot[qs] * sin).astype(BF16)
        kr_t = (k_ref[bi].astype(F32) * cos + rot[ks] * sin).T
        k_bd = jnp.concatenate([kr_t.astype(BF16)] * RET_HEADS, axis=1) * maskk_ref[...]
        a = _dot(qb, k_bd) * decay_ref[...]
        v = v_ref[bi]
        v_bd = jnp.concatenate([v] * RET_HEADS, axis=0) * maskv_ref[...]
        s_bd = state[bi]
        outs.append(_dot(a.astype(BF16), v_bd) + _dot(qb, s_bd.astype(BF16)) * xi_ref[...])
        kz_t = (kr_t * zeta_ref[...]).astype(BF16)
        new_states.append(gmat_ref[...] * s_bd + _dot(kz_t, v) * masks_ref[...])
    o = jnp.concatenate(outs, axis=0)

    def head_mean(x):
        hi = x.astype(BF16)
        lo = (x - hi.astype(F32)).astype(BF16)
        return _dot(hi, mavg_ref[...]) + _dot(lo, mavg_ref[...])

    d = o - head_mean(o)
    y = d * lax.rsqrt(head_mean(d * d) + 1e-5) * gn_ref[...]
    for bi in range(nb):
        gate = g_ref[bi].astype(F32)
        o_ref[bi] = (gate * jax.nn.sigmoid(gate) * y[bi * chunk:(bi + 1) * chunk]).astype(BF16)
        state[bi] = new_states[bi]


def _retention_tables(seq):
    half = HEAD_DIM // 2
    pos = jnp.arange(seq, dtype=F32)
    inv = ROPE_BASE ** (-jnp.arange(half, dtype=F32) / half)
    ang = pos[:, None] * inv[None, :]
    cos, sin = jnp.cos(ang), jnp.sin(ang)
    cos_t = jnp.tile(jnp.concatenate([cos, cos], axis=1), (1, RET_HEADS))
    sin_t = jnp.tile(jnp.concatenate([-sin, sin], axis=1), (1, RET_HEADS))
    log_gamma = jnp.log1p(-jnp.exp2(-5.0 - jnp.arange(RET_HEADS, dtype=F32)))
    i = jnp.arange(RET_CHUNK, dtype=F32)
    rel = i[:, None] - i[None, :]
    decay = jnp.where(rel >= 0, jnp.exp(jnp.maximum(rel, 0.0)[None] * log_gamma[:, None, None]), 0.0)
    decay_cat = decay.transpose(1, 0, 2).reshape(RET_CHUNK, RET_HEADS * RET_CHUNK)
    xi = jnp.exp((i + 1.0)[None, :] * log_gamma[:, None])
    zeta = jnp.exp((RET_CHUNK - 1.0 - i)[None, :] * log_gamma[:, None])
    g_chunk = jnp.exp(RET_CHUNK * log_gamma)
    lanes = lambda a: jnp.repeat(a.T, HEAD_DIM, axis=1)
    j = np.arange(RET_W)
    partner = (j // HEAD_DIM) * HEAD_DIM + (j % HEAD_DIM + half) % HEAD_DIM
    perm = np.zeros((RET_W, RET_W), np.float32)
    perm[partner, j] = 1.0
    head_of_feat = j // HEAD_DIM
    head_of_key = np.arange(RET_HEADS * RET_CHUNK) // RET_CHUNK
    maskk = (head_of_feat[:, None] == head_of_key[None, :]).astype(np.float32)
    masks = (head_of_feat[:, None] == head_of_feat[None, :]).astype(np.float32)
    gmat = jnp.asarray(masks) * jnp.repeat(g_chunk, HEAD_DIM)[:, None]
    return (cos_t, sin_t, jnp.asarray(perm, BF16), decay_cat, lanes(xi), lanes(zeta).T, jnp.asarray(maskk, BF16),
            jnp.asarray(maskk.T, BF16), gmat, jnp.asarray(masks), jnp.asarray(masks / HEAD_DIM, BF16))


def _retention(ret, gn_g, tables, batch, seq):
    nc = seq // RET_CHUNK
    rb = RET_BATCH
    cos_t, sin_t = tables[:2]
    consts = tables[2:]
    ret3 = ret.reshape(batch, seq, 4 * RET_W)
    tile = lambda col: pl.BlockSpec((rb, RET_CHUNK, RET_W), lambda b, c: (b, c, col))
    const = lambda a: pl.BlockSpec(a.shape, lambda b, c: (0,) * a.ndim)
    out = pl.pallas_call(
        _ret_body,
        grid=(batch // rb, nc),
        in_specs=[
            tile(0), tile(1), tile(2), tile(3),
            pl.BlockSpec((RET_CHUNK, RET_W), lambda b, c: (c, 0)),
            pl.BlockSpec((RET_CHUNK, RET_W), lambda b, c: (c, 0)),
            *[const(a) for a in consts],
            pl.BlockSpec((1, RET_W), lambda b, c: (0, 0)),
        ],
        out_specs=pl.BlockSpec((rb, RET_CHUNK, RET_W), lambda b, c: (b, c, 0)),
        out_shape=jax.ShapeDtypeStruct((batch, seq, RET_W), BF16),
        scratch_shapes=[pltpu.VMEM((rb, RET_W, RET_W), F32)],
        compiler_params=_params("parallel", "arbitrary"),
        name="retention",
    )(ret3, ret3, ret3, ret3, cos_t, sin_t, *consts, gn_g.reshape(1, RET_W))
    return out.reshape(batch * seq, RET_W)


def _out_proj_body(*refs, with_router):
    if with_router:
        oa_ref, ob_ref, oc_ref, x_ref, w_ref, g_ref, rcat_ref, xo_ref, h_ref, lg_ref = refs
    else:
        oa_ref, ob_ref, oc_ref, x_ref, w_ref, g_ref, xo_ref, h_ref = refs
    y = (_dot(oa_ref[...], w_ref[0:NSA_Q_W])
         + _dot(ob_ref[...], w_ref[NSA_Q_W:NSA_Q_W + CONV_CH])
         + _dot(oc_ref[...], w_ref[NSA_Q_W + CONV_CH:]))
    x = x_ref[...] + y
    xo_ref[...] = x
    ms = jnp.mean(x * x, axis=-1, keepdims=True)
    h = x * lax.rsqrt(ms + EPS) * g_ref[...]
    hi = h.astype(BF16)
    h_ref[...] = hi
    if with_router:
        lo = (h - hi.astype(F32)).astype(BF16)
        t = _dot(hi, rcat_ref[...])
        lg_ref[...] = t[:, :LANES] + t[:, LANES:] + _dot(lo, rcat_ref[:, :LANES])


def _out_proj(oa, ob, oc, x2, w_out, ln_g, router=None):
    n = x2.shape[0]
    tm = ROW_TILE
    with_router = router is not None
    row = lambda w: pl.BlockSpec((tm, w), lambda i: (i, 0))
    const = lambda shape: pl.BlockSpec(shape, lambda i: (0, 0))
    in_specs = [row(NSA_Q_W), row(CONV_CH), row(RET_W), row(D_MODEL), const((D_MODEL, D_MODEL)), const((1, D_MODEL))]
    args = [oa, ob, oc, x2, w_out.astype(BF16), ln_g.reshape(1, D_MODEL)]
    out_specs = [row(D_MODEL), row(D_MODEL)]
    out_shape = [jax.ShapeDtypeStruct((n, D_MODEL), F32), jax.ShapeDtypeStruct((n, D_MODEL), BF16)]
    if with_router:
        rpad = jnp.pad(router, ((0, 0), (0, LANES - N_EXPERTS)))
        rhi = rpad.astype(BF16)
        rlo = (rpad - rhi.astype(F32)).astype(BF16)
        in_specs += [const((D_MODEL, 2 * LANES))]
        args += [jnp.concatenate([rhi, rlo], axis=1)]
        out_specs.append(row(LANES))
        out_shape.append(jax.ShapeDtypeStruct((n, LANES), F32))
    return pl.pallas_call(
        functools.partial(_out_proj_body, with_router=with_router),
        grid=(n // tm,),
        in_specs=in_specs,
        out_specs=out_specs,
        out_shape=out_shape,
        compiler_params=_params("parallel"),
        name="out_proj_router" if with_router else "out_proj",
    )(*args)


def _ffn_body(*refs, with_residual):
    if with_residual:
        e_ref, nv_ref, x_ref, wa_ref, wb_ref, w2_ref, r_ref, o_ref, acc = refs
    else:
        e_ref, nv_ref, x_ref, wa_ref, wb_ref, w2_ref, o_ref, acc = refs
    del e_ref
    f = pl.program_id(1)
    live = pl.program_id(0) < nv_ref[0]

    @pl.when(f == 0)
    def _():
        acc[...] = jnp.zeros_like(acc)

    @pl.when(live)
    def _():
        x = x_ref[...]
        a = _dot(x, wa_ref[...])
        b = _dot(x, wb_ref[...])
        hmid = (a * jax.nn.sigmoid(a) * b).astype(BF16)
        acc[...] += _dot(hmid, w2_ref[...])

    @pl.when(f == pl.num_programs(1) - 1)
    def _():
        o_ref[...] = (acc[...] + r_ref[...] if with_residual else acc[...]).astype(o_ref.dtype)


def _ffn(xs, blk_e, n_live, w13, w2, rows, out_dtype, residual=None):
    p = xs.shape[0]
    nf = D_FF // FF_TILE
    with_residual = residual is not None
    last = lambda i, nv: jnp.minimum(i, nv[0] - 1)
    in_specs = [
        pl.BlockSpec((rows, D_MODEL), lambda i, f, e, nv: (last(i, nv), 0)),
        pl.BlockSpec((None, D_MODEL, FF_TILE), lambda i, f, e, nv: (e[last(i, nv)], 0, f)),
        pl.BlockSpec((None, D_MODEL, FF_TILE), lambda i, f, e, nv: (e[last(i, nv)], 0, nf + f)),
        pl.BlockSpec((None, FF_TILE, D_MODEL), lambda i, f, e, nv: (e[last(i, nv)], f, 0)),
    ]
    args = [xs, w13, w13, w2]
    if with_residual:
        in_specs.append(pl.BlockSpec((rows, D_MODEL), lambda i, f, e, nv: (i, 0)))
        args.append(residual)
    return pl.pallas_call(
        functools.partial(_ffn_body, with_residual=with_residual),
        grid_spec=pltpu.PrefetchScalarGridSpec(
            num_scalar_prefetch=2,
            grid=(p // rows, nf),
            in_specs=in_specs,
            out_specs=pl.BlockSpec((rows, D_MODEL), lambda i, f, e, nv: (i, 0)),
            scratch_shapes=[pltpu.VMEM((rows, D_MODEL), F32)],
        ),
        out_shape=jax.ShapeDtypeStruct((p, D_MODEL), out_dtype),
        compiler_params=_params("parallel", "arbitrary"),
        name="ffn_residual" if with_residual else "ffn_grouped",
    )(blk_e, n_live, *args)


def _final_body(x_ref, y0_ref, y1_ref, gt_ref, g_ref, o_ref):
    gt = gt_ref[...]
    x = x_ref[...] + gt[:, 0:1] * y0_ref[...].astype(F32) + gt[:, 1:2] * y1_ref[...].astype(F32)
    ms = jnp.mean(x * x, axis=-1, keepdims=True)
    o_ref[...] = x * lax.rsqrt(ms + EPS) * g_ref[...]


def _final(x2, y0, y1, gates_pad, ln_g):
    n = x2.shape[0]
    tm = ROW_TILE
    row = lambda w: pl.BlockSpec((tm, w), lambda i: (i, 0))
    return pl.pallas_call(
        _final_body,
        grid=(n // tm,),
        in_specs=[row(D_MODEL), row(D_MODEL), row(D_MODEL), row(LANES), pl.BlockSpec((1, D_MODEL), lambda i: (0, 0))],
        out_specs=row(D_MODEL),
        out_shape=jax.ShapeDtypeStruct((n, D_MODEL), F32),
        compiler_params=_params("parallel"),
        name="moe_combine_final_norm",
    )(x2, y0, y1, gates_pad, ln_g.reshape(1, D_MODEL))


def _overlap_t(seq):
    nc = seq // CMP_STRIDE
    ns = seq // SLC_BLOCK
    cs = np.arange(nc)[None, :] * CMP_STRIDE
    ss = np.arange(ns)[:, None] * SLC_BLOCK
    return jnp.asarray(((cs < ss + SLC_BLOCK) & (cs + CMP_LEN > ss)).astype(np.float32), BF16)


def _route(logits, rows):
    n = logits.shape[0]
    top_logit, top_e = lax.top_k(logits, TOP_K)
    gates = jax.nn.softmax(top_logit, axis=-1)
    e_flat = top_e.reshape(-1).astype(jnp.int32)
    onehot = (e_flat[:, None] == jnp.arange(N_EXPERTS, dtype=jnp.int32)[None, :]).astype(jnp.int32)
    csum = jnp.cumsum(onehot, axis=0)
    counts = csum[-1]
    padded = ((counts + rows - 1) // rows) * rows
    starts = jnp.cumsum(counts) - counts
    pends = jnp.cumsum(padded)
    pstarts = pends - padded
    pos = jnp.sum(onehot * (csum - 1 + pstarts[None, :]), axis=1).reshape(n, TOP_K)
    p = n * TOP_K + N_EXPERTS * rows
    nblk = p // rows
    blk_e = jnp.minimum(jnp.searchsorted(pends, jnp.arange(nblk) * rows, side='right'), N_EXPERTS - 1)
    blk_e = blk_e.astype(jnp.int32)
    order = jnp.argsort(e_flat, stable=True).astype(jnp.int32)
    slot_e = jnp.repeat(blk_e, rows)
    off = jnp.arange(p, dtype=jnp.int32) - pstarts[slot_e]
    live = (off < counts[slot_e]) & (jnp.arange(p) < pends[-1])
    src = jnp.clip(starts[slot_e] + off, 0, n * TOP_K - 1)
    buf_tok = jnp.where(live, order[src] // TOP_K, 0).astype(jnp.int32)
    n_live = (pends[-1:] // rows).astype(jnp.int32)
    return gates, buf_tok, pos, blk_e, n_live


def _token_mixer(x2, l, batch, seq, tables, ovt, ln_attn, w_in, cmp_pe, cmp_w1, cmp_w2, conv_w, conv_b, conv_ln_g,
                 conv_ln_b, conv_pw, ret_gn):
    hm, ck, wd, gates, conv_in, ret = _norm_proj(x2, ln_attn[l], _arrange_w_in(w_in[l]), seq)
    nsub = seq // CMP_STRIDE
    half = CMP_STRIDE * HEAD_DIM
    sub = ck.reshape(CK_SLABS, batch, nsub, half)
    w1 = cmp_w1[l]
    w1cat = jnp.concatenate([w1[:, :half], w1[:, half:]], axis=2).astype(BF16)
    pe8 = jnp.broadcast_to(cmp_pe[l].reshape(2, 1, CMP_LEN * HEAD_DIM), (2, 8, CMP_LEN * HEAD_DIM)).astype(BF16)
    kvc = _compress(sub, pe8, w1cat, cmp_w2[l].astype(BF16))
    o_a = _nsa(hm, wd, gates, kvc, ovt, batch, seq)
    w32 = jnp.pad(conv_w[l], ((0, 32 - CONV_K), (0, 0)))
    o_b = _conv(conv_in, w32, conv_b[l], conv_ln_g[l], conv_ln_b[l], conv_pw[l].astype(BF16), batch, seq)
    o_c = _retention(ret, ret_gn[l], tables, batch, seq)
    return o_a, o_b, o_c


def kernel(x, ln_attn, w_in, cmp_pe, cmp_w1, cmp_w2, conv_w, conv_b, conv_ln_g, conv_ln_b, conv_pw, ret_gn, w_out,
           ln_ffn, ffn_w13, ffn_w2, router, moe_w13, moe_w2, ln_final):
    batch, seq, d = x.shape
    n = batch * seq
    depth = ln_attn.shape[0]
    x2 = x.reshape(n, d)
    tables = _retention_tables(seq)
    ovt = _overlap_t(seq)
    out = None
    for l in range(depth):
        o_a, o_b, o_c = _token_mixer(x2, l, batch, seq, tables, ovt, ln_attn, w_in, cmp_pe, cmp_w1, cmp_w2, conv_w,
                                     conv_b, conv_ln_g, conv_ln_b, conv_pw, ret_gn)
        if l % 2 == 0:
            x2, h2 = _out_proj(o_a, o_b, o_c, x2, w_out[l], ln_ffn[l])
            nblk = n // FFN_ROWS
            x2 = _ffn(h2, jnp.zeros((nblk,), jnp.int32), jnp.full((1,), nblk, jnp.int32),
                      ffn_w13[l // 2][None].astype(BF16), ffn_w2[l // 2][None].astype(BF16), FFN_ROWS, F32, residual=x2)
        else:
            x2, h2, logits = _out_proj(o_a, o_b, o_c, x2, w_out[l], ln_ffn[l], router=router[l // 2])
            gates, buf_tok, pos, blk_e, n_live = _route(logits[:, :N_EXPERTS], MOE_ROWS)
            xs = h2[buf_tok]
            yb = _ffn(xs, blk_e, n_live, moe_w13[l // 2].astype(BF16), moe_w2[l // 2].astype(BF16), MOE_ROWS, BF16)
            gates_pad = jnp.pad(gates, ((0, 0), (0, LANES - TOP_K)))
            if l == depth - 1:
                out = _final(x2, yb[pos[:, 0]], yb[pos[:, 1]], gates_pad, ln_final)
            else:
                x2 = x2 + gates[:, 0:1] * yb[pos[:, 0]] + gates[:, 1:2] * yb[pos[:, 1]]
    if out is None:
        out = _final(x2, jnp.zeros_like(x2), jnp.zeros_like(x2), jnp.zeros((n, LANES), F32), ln_final)
    return out.reshape(batch, seq, d)
```

```python
import functools

import numpy as np
import jax
import jax.numpy as jnp
from jax import lax
from jax.experimental import pallas as pl
from jax.experimental.pallas import tpu as pltpu

D_MODEL = 1024
HEAD_DIM = 64
NSA_HEADS = 8
NSA_KV_HEADS = 2
NSA_GROUP = NSA_HEADS // NSA_KV_HEADS
CMP_LEN = 32
CMP_STRIDE = 16
CMP_HIDDEN = 256
SLC_BLOCK = 64
SLC_TOPK = 16
WINDOW = 512
N_BRANCH = 3
CONV_CH = 256
CONV_K = 31
RET_HEADS = 4
RET_CHUNK = 128
ROPE_BASE = 10000.0
D_FF = 2816
N_EXPERTS = 8
TOP_K = 2
NEG_INF = -1e30
FORCE_SCORE = 1e9
EPS = 1e-6

NSA_Q_W = NSA_HEADS * HEAD_DIM
NSA_KV_W = NSA_KV_HEADS * HEAD_DIM
NSA_GATE_W = NSA_HEADS * N_BRANCH
RET_W = RET_HEADS * HEAD_DIM

LANES = 128
SUBLANES = 8
Q_TILE = 512
WIN_TILE = 256
ROW_BLK = 32
SEL_CHUNK = 512
ROW_TILE = 512
RET_BATCH = 8
CONV_TILE = 512
CONV_HALO = 32
CONV_TAP_ROWS = 32
MOE_ROWS = 512
FFN_ROWS = 512
FF_TILE = 1408
V7X_VMEM_BYTES = 64 * 1024 * 1024
VMEM_LIMIT = V7X_VMEM_BYTES * 7 // 8

BF16 = jnp.bfloat16
F32 = jnp.float32

HM_Q, HM_KW, HM_SLABS = 0, 8, 10
CK_SLABS = 4
WD_KS, WD_VS, WD_VW, WD_SLABS = 0, 2, 4, 6
LOG2E = 1.4426950408889634


def _dot(a, b):
    return jnp.dot(a, b, preferred_element_type=F32)


def _dot_nt(a, b):
    return lax.dot_general(a, b, (((1,), (1,)), ((), ())), preferred_element_type=F32)


def _split3(x):
    hi = x.astype(BF16)
    r1 = x - hi.astype(F32)
    mid = r1.astype(BF16)
    lo = (r1 - mid.astype(F32)).astype(BF16)
    return hi, mid, lo


def _params(*sem):
    return pltpu.CompilerParams(dimension_semantics=sem, vmem_limit_bytes=VMEM_LIMIT)


W_HM = (HM_SLABS + CK_SLABS) * HEAD_DIM
W_WD = W_HM + 3 * NSA_KV_W
W_GATE = W_WD + 2 * LANES
W_CONV = W_GATE + 2 * CONV_CH
W_RET = W_CONV + 4 * RET_W


def _arrange_w_in(w_in):
    offs = np.cumsum([0, NSA_Q_W] + [NSA_KV_W] * 6 + [NSA_GATE_W, 2 * CONV_CH] + [RET_W] * 4)
    q, kc, vc, ks, vs, kw, vw, gl, conv, rq, rk, rv, rg = [w_in[:, offs[i]:offs[i + 1]] for i in range(13)]
    scale = HEAD_DIM ** -0.5
    gpad = jnp.zeros((D_MODEL, LANES - NSA_GATE_W // 2), w_in.dtype)
    gates = jnp.concatenate([gl[:, :NSA_GATE_W // 2], gpad, gl[:, NSA_GATE_W // 2:], gpad], axis=1)
    w = jnp.concatenate([q * (scale * LOG2E), kw, kc, vc, ks, vs, vw, gates, conv, rq, rk * scale, rv, rg], axis=1)
    return w.astype(BF16)


def _norm_proj_body(x_ref, g_ref, w_ref, hm_ref, ck_ref, wd_ref, gates_ref, conv_ref, ret_ref, *, seq):
    x = x_ref[...]
    ms = jnp.mean(x * x, axis=-1, keepdims=True)
    h = (x * lax.rsqrt(ms + EPS) * g_ref[...]).astype(BF16)
    res = _dot(h, w_ref[:, 0:W_HM])
    for j in range(HM_SLABS + CK_SLABS):
        dst = hm_ref.at[j] if j < HM_SLABS else ck_ref.at[j - HM_SLABS]
        dst[...] = res[:, j * HEAD_DIM:(j + 1) * HEAD_DIM].astype(BF16)
    rows = x.shape[0]
    t0 = (pl.program_id(0) % (seq // rows)) * rows
    lane = lax.broadcasted_iota(jnp.int32, (rows, HEAD_DIM), 1)
    blk = (t0 + lax.broadcasted_iota(jnp.int32, (rows, HEAD_DIM), 0)) // SLC_BLOCK
    onehot = jnp.where(blk == lane, 1.0, 0.0)
    ones_col = jnp.where(lane == 0, 1.0, 0.0)
    wide = _dot(h, w_ref[:, W_HM:W_WD])
    for j, ext in ((WD_KS, onehot), (WD_VS, ones_col), (WD_VW, ones_col)):
        for g in range(NSA_KV_HEADS):
            c = (j + g) * HEAD_DIM
            wd_ref[j + g] = jnp.concatenate([wide[:, c:c + HEAD_DIM], ext], axis=1).astype(BF16)
    gates_ref[...] = _dot(h, w_ref[:, W_WD:W_GATE])
    conv_ref[...] = _dot(h, w_ref[:, W_GATE:W_CONV]).astype(BF16)
    ret_ref[...] = _dot(h, w_ref[:, W_CONV:W_RET]).astype(BF16)


def _norm_proj(x2, ln_g, w_arr, seq):
    n = x2.shape[0]
    tm = ROW_TILE
    return pl.pallas_call(
        functools.partial(_norm_proj_body, seq=seq),
        grid=(n // tm,),
        in_specs=[
            pl.BlockSpec((tm, D_MODEL), lambda i: (i, 0)),
            pl.BlockSpec((1, D_MODEL), lambda i: (0, 0)),
            pl.BlockSpec((D_MODEL, W_RET), lambda i: (0, 0)),
        ],
        out_specs=[
            pl.BlockSpec((HM_SLABS, tm, HEAD_DIM), lambda i: (0, i, 0)),
            pl.BlockSpec((CK_SLABS, tm, HEAD_DIM), lambda i: (0, i, 0)),
            pl.BlockSpec((WD_SLABS, tm, LANES), lambda i: (0, i, 0)),
            pl.BlockSpec((tm, 2 * LANES), lambda i: (i, 0)),
            pl.BlockSpec((tm, 2 * CONV_CH), lambda i: (i, 0)),
            pl.BlockSpec((tm, 4 * RET_W), lambda i: (i, 0)),
        ],
        out_shape=[
            jax.ShapeDtypeStruct((HM_SLABS, n, HEAD_DIM), BF16),
            jax.ShapeDtypeStruct((CK_SLABS, n, HEAD_DIM), BF16),
            jax.ShapeDtypeStruct((WD_SLABS, n, LANES), BF16),
            jax.ShapeDtypeStruct((n, 2 * LANES), F32),
            jax.ShapeDtypeStruct((n, 2 * CONV_CH), BF16),
            jax.ShapeDtypeStruct((n, 4 * RET_W), BF16),
        ],
        compiler_params=_params("parallel"),
        name="norm_proj",
    )(x2, ln_g.reshape(1, D_MODEL), w_arr)


def _compress_body(s_ref, pe_ref, w1_ref, w2_ref, o_ref):
    u = _dot(s_ref[...], w1_ref[...])
    pe = pe_ref[...]
    bias = (_dot(pe[:, :CMP_STRIDE * HEAD_DIM], w1_ref[:, :CMP_HIDDEN])
            + _dot(pe[:, CMP_STRIDE * HEAD_DIM:], w1_ref[:, CMP_HIDDEN:]))[0:1]
    bot = u[:, CMP_HIDDEN:]
    bot_next = jnp.concatenate([bot[1:], bot[:1]], axis=0)
    hid = u[:, :CMP_HIDDEN] + bot_next + bias
    hid = hid * jax.nn.sigmoid(hid)
    o_ref[...] = _dot(hid.astype(BF16), w2_ref[...]).astype(BF16)


def _compress(sub, pe8, w1cat, w2):
    _, b, nsub, width = sub.shape
    return pl.pallas_call(
        _compress_body,
        grid=(4, b),
        in_specs=[
            pl.BlockSpec((None, None, nsub, width), lambda j, i: (j, i, 0, 0)),
            pl.BlockSpec((None, 8, 2 * width), lambda j, i: (j // 2, 0, 0)),
            pl.BlockSpec((None, width, 2 * CMP_HIDDEN), lambda j, i: (j // 2, 0, 0)),
            pl.BlockSpec((None, CMP_HIDDEN, HEAD_DIM), lambda j, i: (j // 2, 0, 0)),
        ],
        out_specs=pl.BlockSpec((None, None, nsub, HEAD_DIM), lambda j, i: (j, i, 0, 0)),
        out_shape=jax.ShapeDtypeStruct((4, b, nsub, HEAD_DIM), BF16),
        compiler_params=_params("parallel", "parallel"),
        name="compress",
    )(sub, pe8, w1cat, w2)


def _topk_block_bias(imp_t):
    ns, nq = imp_t.shape
    sub = SUBLANES
    nb = ns // sub
    blocks = [imp_t[v * sub:(v + 1) * sub] for v in range(nb)]
    cnts = [jnp.zeros((sub, nq), jnp.int32) for _ in range(nb)]
    sub_iota = lax.broadcasted_iota(jnp.int32, (sub, nq), 0)
    for i in range(ns):
        row = jnp.broadcast_to(imp_t[i:i + 1], (sub, nq))
        for v in range(nb):
            if v > i // sub:
                cnts[v] = cnts[v] + jnp.where(row >= blocks[v], 1, 0)
            elif v < i // sub:
                cnts[v] = cnts[v] + jnp.where(row > blocks[v], 1, 0)
            else:
                tie = (row == blocks[v]) & (sub_iota > i % sub)
                cnts[v] = cnts[v] + jnp.where(row > blocks[v], 1, 0) + jnp.where(tie, 1, 0)
    cnt = jnp.concatenate(cnts, axis=0)
    return jnp.where(cnt < SLC_TOPK, 0.0, NEG_INF)


def _flash_update(qs, k, v1, bias, state):
    nh = len(state)
    qn = qs.shape[0] // nh
    s_all = _dot_nt(qs, k)
    ps, ms, alphas = [], [], []
    for h in range(nh):
        s = s_all[h * qn:(h + 1) * qn]
        if bias is not None:
            s = s + bias
        m = state[h][0]
        m_new = jnp.maximum(m, jnp.max(s, axis=-1, keepdims=True))
        ps.append(jnp.exp2(s - m_new).astype(BF16))
        alphas.append(jnp.exp2(m - m_new))
        ms.append(m_new)
    pv = _dot(jnp.concatenate(ps, axis=0), v1)
    return tuple((ms[h], alphas[h] * state[h][1] + pv[h * qn:(h + 1) * qn]) for h in range(nh))


def _flash_update_staged(qs, k, v1, bias, nh, s_scr, p_scr, m_scr, al_scr, acc_scr):
    rows, width = qs.shape[0], k.shape[0]
    qn = rows // nh
    s_scr[...] = _dot_nt(qs, k)
    for b in range(rows // ROW_BLK):
        r = slice(b * ROW_BLK, (b + 1) * ROW_BLK)
        q0 = (b * ROW_BLK) % qn
        s = s_scr[r] + bias[q0:q0 + ROW_BLK]
        m_old = m_scr[r]
        m_new = jnp.maximum(m_old, jnp.broadcast_to(jnp.max(s, axis=-1, keepdims=True), (ROW_BLK, LANES)))
        al_scr[r] = jnp.exp2(m_old - m_new)
        m_scr[r] = m_new
        p_scr[r] = jnp.exp2(s - jnp.concatenate([m_new] * (width // LANES), axis=1)).astype(BF16)
    acc_scr[...] = al_scr[...] * acc_scr[...] + _dot(p_scr[...], v1)


def _nsa_body(q_ref, gl_ref, kc_ref, vc_ref, ks_ref, vs_ref, kw_ref, vw_ref, ovt_ref, o_ref,
              s_scr, p_scr, m_sel, al_sel, acc_sel):
    i = pl.program_id(2)
    hpg = NSA_GROUP
    q = q_ref[...].reshape(hpg * Q_TILE, HEAD_DIM)
    rr = lax.broadcasted_iota(jnp.int32, (Q_TILE, 1), 0)
    t_q = i * Q_TILE + rr
    heads = [slice(h * Q_TILE, (h + 1) * Q_TILE) for h in range(hpg)]

    kc = kc_ref[...]
    ncb = kc.shape[0]
    s_c = _dot_nt(q, kc)
    c_end = lax.broadcasted_iota(jnp.int32, (1, ncb), 1) * CMP_STRIDE + (CMP_LEN - 1)
    cbias = jnp.where(c_end <= t_q, 0.0, NEG_INF)
    some_block = t_q >= CMP_LEN - 1
    pcs, psum = [], None
    for r in heads:
        s = s_c[r] + cbias
        p = jnp.exp2(s - jnp.max(s, axis=-1, keepdims=True))
        l = jnp.sum(p, axis=-1, keepdims=True)
        p = p * jnp.where(some_block, 1.0 / l, 0.0)
        pcs.append(p.astype(BF16))
        psum = p if psum is None else psum + p
    o_cmp = _dot(jnp.concatenate(pcs, axis=0), vc_ref[...])

    ovt = ovt_ref[...]
    imp_t = sum(_dot_nt(ovt, part) for part in _split3(psum))
    ns = imp_t.shape[0]
    sblk = lax.broadcasted_iota(jnp.int32, (ns, Q_TILE), 0)
    cur = (i * Q_TILE + lax.broadcasted_iota(jnp.int32, (ns, Q_TILE), 1)) // SLC_BLOCK
    forced = (sblk == 0) | (sblk == cur) | (sblk == cur - 1)
    imp_t = jnp.where(sblk <= cur, jnp.where(forced, FORCE_SCORE, imp_t), NEG_INF)
    few_blocks = (i + 1) * Q_TILE <= SLC_TOPK * SLC_BLOCK
    bias_t = lax.cond(few_blocks, lambda x: jnp.where(x > 0.5 * NEG_INF, 0.0, NEG_INF), _topk_block_bias, imp_t)
    bias_pad = jnp.concatenate([bias_t, jnp.zeros((LANES - ns, Q_TILE), F32)], axis=0).T
    bias = bias_pad[:, :ns].astype(BF16)
    qp = jnp.concatenate([q, jnp.concatenate([bias] * hpg, axis=0)], axis=1)

    m_sel[...] = jnp.full(m_sel.shape, NEG_INF, F32)
    acc_sel[...] = jnp.zeros(acc_sel.shape, F32)
    n_full = (i * Q_TILE) // SEL_CHUNK

    def sel_step(c, carry):
        k0 = pl.multiple_of(c * SEL_CHUNK, SEL_CHUNK)
        kpos = k0 + lax.broadcasted_iota(jnp.int32, (1, SEL_CHUNK), 1)
        causal = jnp.where(kpos <= t_q, 0.0, NEG_INF)
        _flash_update_staged(qp, ks_ref[pl.ds(k0, SEL_CHUNK), :], vs_ref[pl.ds(k0, SEL_CHUNK), :], causal, hpg,
                             s_scr, p_scr, m_sel, al_sel, acc_sel)
        return carry

    lax.fori_loop(0, n_full + 1, sel_step, 0)

    wkeys = WINDOW + WIN_TILE
    init_w = tuple((jnp.full((WIN_TILE, 1), NEG_INF, F32), jnp.zeros((WIN_TILE, LANES), F32)) for _ in range(hpg))
    st_w = []
    for u in range(Q_TILE // WIN_TILE):
        sub = slice(u * WIN_TILE, (u + 1) * WIN_TILE)
        q_u = jnp.concatenate([q[r][sub] for r in heads], axis=0)
        w0 = pl.multiple_of(jnp.maximum(i * Q_TILE + u * WIN_TILE - WINDOW, 0), WIN_TILE)
        dpos = t_q[sub] - (w0 + lax.broadcasted_iota(jnp.int32, (1, wkeys), 1))
        band = jnp.where((dpos >= 0) & (dpos < WINDOW), 0.0, NEG_INF)
        st_w.append(_flash_update(q_u, kw_ref[pl.ds(w0, wkeys), :], vw_ref[pl.ds(w0, wkeys), :], band, init_w))

    sig = jax.nn.sigmoid(gl_ref[...])
    outs = []
    for h, r in enumerate(heads):
        c = h * N_BRANCH
        acc_s = acc_sel[r]
        acc_w = jnp.concatenate([st_u[h][1] for st_u in st_w], axis=0)
        g_s = sig[:, c + 1:c + 2] / acc_s[:, HEAD_DIM:HEAD_DIM + 1]
        g_w = sig[:, c + 2:c + 3] / acc_w[:, HEAD_DIM:HEAD_DIM + 1]
        outs.append(sig[:, c:c + 1] * o_cmp[r] + g_s * acc_s[:, :HEAD_DIM] + g_w * acc_w[:, :HEAD_DIM])
    o_ref[...] = jnp.concatenate(outs, axis=1).astype(BF16)


def _nsa(hm, wd, gates, kvc, ovt, batch, seq):
    n = batch * seq
    nq = seq // Q_TILE
    ncb = kvc.shape[2]
    g_ = NSA_KV_HEADS
    rows = NSA_GROUP * Q_TILE
    return pl.pallas_call(
        _nsa_body,
        grid=(batch, g_, nq),
        in_specs=[
            pl.BlockSpec((NSA_GROUP, Q_TILE, HEAD_DIM), lambda b, g, i: (g, b * nq + i, 0)),
            pl.BlockSpec((Q_TILE, LANES), lambda b, g, i: (b * nq + i, g)),
            pl.BlockSpec((None, None, ncb, HEAD_DIM), lambda b, g, i: (g, b, 0, 0)),
            pl.BlockSpec((None, None, ncb, HEAD_DIM), lambda b, g, i: (g_ + g, b, 0, 0)),
            pl.BlockSpec((None, seq, LANES), lambda b, g, i: (WD_KS + g, b, 0)),
            pl.BlockSpec((None, seq, LANES), lambda b, g, i: (WD_VS + g, b, 0)),
            pl.BlockSpec((None, seq, HEAD_DIM), lambda b, g, i: (HM_KW + g, b, 0)),
            pl.BlockSpec((None, seq, LANES), lambda b, g, i: (WD_VW + g, b, 0)),
            pl.BlockSpec(ovt.shape, lambda b, g, i: (0, 0)),
        ],
        out_specs=pl.BlockSpec((Q_TILE, NSA_GROUP * HEAD_DIM), lambda b, g, i: (b * nq + i, g)),
        out_shape=jax.ShapeDtypeStruct((n, NSA_Q_W), BF16),
        scratch_shapes=[
            pltpu.VMEM((rows, SEL_CHUNK), F32), pltpu.VMEM((rows, SEL_CHUNK), BF16),
            pltpu.VMEM((rows, LANES), F32), pltpu.VMEM((rows, LANES), F32), pltpu.VMEM((rows, LANES), F32),
        ],
        compiler_params=_params("parallel", "parallel", "arbitrary"),
        name="nsa",
    )(hm, gates, kvc, kvc, wd, wd, hm, wd, ovt)


def _conv_body(u_ref, up_ref, w_ref, b_ref, lg_ref, lb_ref, pw_ref, o_ref, hbuf, sbuf):
    i = pl.program_id(1)

    def glu(u):
        u = u.astype(F32)
        return u[:, :CONV_CH] * jax.nn.sigmoid(u[:, CONV_CH:])

    prev = glu(up_ref[...])
    hbuf[0:CONV_HALO] = jnp.where(i > 0, prev, 0.0)
    hbuf[CONV_HALO:] = glu(u_ref[...])
    span = sbuf.shape[1]
    for s in range(1, SUBLANES):
        sbuf[s - 1] = hbuf[s:s + span]
    w = w_ref[...]
    acc = jnp.zeros((CONV_TILE, CONV_CH), F32) + b_ref[...]
    base = CONV_HALO - (CONV_K - 1)
    for j in range(CONV_K):
        s = (base + j) % SUBLANES
        a8 = base + j - s
        src = hbuf[a8:a8 + CONV_TILE] if s == 0 else sbuf[s - 1, a8:a8 + CONV_TILE]
        acc = acc + w[j:j + 1] * src
    mu = jnp.mean(acc, axis=-1, keepdims=True)
    var = jnp.mean(jnp.square(acc - mu), axis=-1, keepdims=True)
    hf = (acc - mu) * lax.rsqrt(var + 1e-5) * lg_ref[...] + lb_ref[...]
    hf = hf * jax.nn.sigmoid(hf)
    o_ref[...] = _dot(hf.astype(BF16), pw_ref[...]).astype(BF16)


def _conv(conv_in, w_taps, b, lg, lb, pw, batch, seq):
    n = batch * seq
    nt = seq // CONV_TILE
    halo_per_tile = CONV_TILE // CONV_HALO
    row = lambda v: v.reshape(1, CONV_CH)
    return pl.pallas_call(
        _conv_body,
        grid=(batch, nt),
        in_specs=[
            pl.BlockSpec((CONV_TILE, 2 * CONV_CH), lambda b_, i: (b_ * nt + i, 0)),
            pl.BlockSpec((CONV_HALO, 2 * CONV_CH),
                         lambda b_, i: (jnp.maximum((b_ * nt + i) * halo_per_tile - 1, 0), 0)),
            pl.BlockSpec((CONV_TAP_ROWS, CONV_CH), lambda b_, i: (0, 0)),
            pl.BlockSpec((1, CONV_CH), lambda b_, i: (0, 0)),
            pl.BlockSpec((1, CONV_CH), lambda b_, i: (0, 0)),
            pl.BlockSpec((1, CONV_CH), lambda b_, i: (0, 0)),
            pl.BlockSpec((CONV_CH, CONV_CH), lambda b_, i: (0, 0)),
        ],
        out_specs=pl.BlockSpec((CONV_TILE, CONV_CH), lambda b_, i: (b_ * nt + i, 0)),
        out_shape=jax.ShapeDtypeStruct((n, CONV_CH), BF16),
        scratch_shapes=[pltpu.VMEM((CONV_HALO + CONV_TILE, CONV_CH), F32),
                        pltpu.VMEM((SUBLANES - 1, CONV_HALO + CONV_TILE - SUBLANES, CONV_CH), F32)],
        compiler_params=_params("parallel", "parallel"),
        name="conformer_conv",
    )(conv_in, conv_in, w_taps, row(b), row(lg), row(lb), pw)


def _ret_body(q_ref, k_ref, v_ref, g_ref, cos_ref, sin_ref, perm_ref, decay_ref, xi_ref, zeta_ref, maskk_ref,
              maskv_ref, gmat_ref, masks_ref, mavg_ref, gn_ref, o_ref, state):
    c = pl.program_id(1)

    @pl.when(c == 0)
    def _():
        state[...] = jnp.zeros_like(state)

    nb = q_ref.shape[0]
    chunk = q_ref.shape[1]
    cos, sin = cos_ref[...], sin_ref[...]
    qk = jnp.concatenate([q_ref[bi] for bi in range(nb)] + [k_ref[bi] for bi in range(nb)], axis=0)
    rot = _dot(qk, perm_ref[...])
    outs, new_states = [], []
    for bi in range(nb):
        qs, ks = slice(bi * chunk, (bi + 1) * chunk), slice((nb + bi) * chunk, (nb + bi + 1) * chunk)
        qb = (q_ref[bi].astype(F32) * cos + rot[qs] * sin).astype(BF16)
        kr_t = (k_ref[bi].astype(F32) * cos + rot[ks] * sin).T
        k_bd = jnp.concatenate([kr_t.astype(BF16)] * RET_HEADS, axis=1) * maskk_ref[...]
        a = _dot(qb, k_bd) * decay_ref[...]
        v = v_ref[bi]
        v_bd = jnp.concatenate([v] * RET_HEADS, axis=0) * maskv_ref[...]
        s_bd = state[bi]
        outs.append(_dot(a.astype(BF16), v_bd) + _dot(qb, s_bd.astype(BF16)) * xi_ref[...])
        kz_t = (kr_t * zeta_ref[...]).astype(BF16)
        new_states.append(gmat_ref[...] * s_bd + _dot(kz_t, v) * masks_ref[...])
    o = jnp.concatenate(outs, axis=0)

    def head_mean(x):
        hi = x.astype(BF16)
        lo = (x - hi.astype(F32)).astype(BF16)
        return _dot(hi, mavg_ref[...]) + _dot(lo, mavg_ref[...])

    d = o - head_mean(o)
    y = d * lax.rsqrt(head_mean(d * d) + 1e-5) * gn_ref[...]
    for bi in range(nb):
        gate = g_ref[bi].astype(F32)
        o_ref[bi] = (gate * jax.nn.sigmoid(gate) * y[bi * chunk:(bi + 1) * chunk]).astype(BF16)
        state[bi] = new_states[bi]


def _retention_tables(seq):
    half = HEAD_DIM // 2
    pos = jnp.arange(seq, dtype=F32)
    inv = ROPE_BASE ** (-jnp.arange(half, dtype=F32) / half)
    ang = pos[:, None] * inv[None, :]
    cos, sin = jnp.cos(ang), jnp.sin(ang)
    cos_t = jnp.tile(jnp.concatenate([cos, cos], axis=1), (1, RET_HEADS))
    sin_t = jnp.tile(jnp.concatenate([-sin, sin], axis=1), (1, RET_HEADS))
    log_gamma = jnp.log1p(-jnp.exp2(-5.0 - jnp.arange(RET_HEADS, dtype=F32)))
    i = jnp.arange(RET_CHUNK, dtype=F32)
    rel = i[:, None] - i[None, :]
    decay = jnp.where(rel >= 0, jnp.exp(jnp.maximum(rel, 0.0)[None] * log_gamma[:, None, None]), 0.0)
    decay_cat = decay.transpose(1, 0, 2).reshape(RET_CHUNK, RET_HEADS * RET_CHUNK)
    xi = jnp.exp((i + 1.0)[None, :] * log_gamma[:, None])
    zeta = jnp.exp((RET_CHUNK - 1.0 - i)[None, :] * log_gamma[:, None])
    g_chunk = jnp.exp(RET_CHUNK * log_gamma)
    lanes = lambda a: jnp.repeat(a.T, HEAD_DIM, axis=1)
    j = np.arange(RET_W)
    partner = (j // HEAD_DIM) * HEAD_DIM + (j % HEAD_DIM + half) % HEAD_DIM
    perm = np.zeros((RET_W, RET_W), np.float32)
    perm[partner, j] = 1.0
    head_of_feat = j // HEAD_DIM
    head_of_key = np.arange(RET_HEADS * RET_CHUNK) // RET_CHUNK
    maskk = (head_of_feat[:, None] == head_of_key[None, :]).astype(np.float32)
    masks = (head_of_feat[:, None] == head_of_feat[None, :]).astype(np.float32)
    gmat = jnp.asarray(masks) * jnp.repeat(g_chunk, HEAD_DIM)[:, None]
    return (cos_t, sin_t, jnp.asarray(perm, BF16), decay_cat, lanes(xi), lanes(zeta).T, jnp.asarray(maskk, BF16),
            jnp.asarray(maskk.T, BF16), gmat, jnp.asarray(masks), jnp.asarray(masks / HEAD_DIM, BF16))


def _retention(ret, gn_g, tables, batch, seq):
    nc = seq // RET_CHUNK
    rb = RET_BATCH
    cos_t, sin_t = tables[:2]
    consts = tables[2:]
    ret3 = ret.reshape(batch, seq, 4 * RET_W)
    tile = lambda col: pl.BlockSpec((rb, RET_CHUNK, RET_W), lambda b, c: (b, c, col))
    const = lambda a: pl.BlockSpec(a.shape, lambda b, c: (0,) * a.ndim)
    out = pl.pallas_call(
        _ret_body,
        grid=(batch // rb, nc),
        in_specs=[
            tile(0), tile(1), tile(2), tile(3),
            pl.BlockSpec((RET_CHUNK, RET_W), lambda b, c: (c, 0)),
            pl.BlockSpec((RET_CHUNK, RET_W), lambda b, c: (c, 0)),
            *[const(a) for a in consts],
            pl.BlockSpec((1, RET_W), lambda b, c: (0, 0)),
        ],
        out_specs=pl.BlockSpec((rb, RET_CHUNK, RET_W), lambda b, c: (b, c, 0)),
        out_shape=jax.ShapeDtypeStruct((batch, seq, RET_W), BF16),
        scratch_shapes=[pltpu.VMEM((rb, RET_W, RET_W), F32)],
        compiler_params=_params("parallel", "arbitrary"),
        name="retention",
    )(ret3, ret3, ret3, ret3, cos_t, sin_t, *consts, gn_g.reshape(1, RET_W))
    return out.reshape(batch * seq, RET_W)


def _out_proj_body(*refs, with_router):
    if with_router:
        oa_ref, ob_ref, oc_ref, x_ref, w_ref, g_ref, rcat_ref, xo_ref, h_ref, lg_ref = refs
    else:
        oa_ref, ob_ref, oc_ref, x_ref, w_ref, g_ref, xo_ref, h_ref = refs
    y = (_dot(oa_ref[...], w_ref[0:NSA_Q_W])
         + _dot(ob_ref[...], w_ref[NSA_Q_W:NSA_Q_W + CONV_CH])
         + _dot(oc_ref[...], w_ref[NSA_Q_W + CONV_CH:]))
    x = x_ref[...] + y
    xo_ref[...] = x
    ms = jnp.mean(x * x, axis=-1, keepdims=True)
    h = x * lax.rsqrt(ms + EPS) * g_ref[...]
    hi = h.astype(BF16)
    h_ref[...] = hi
    if with_router:
        lo = (h - hi.astype(F32)).astype(BF16)
        t = _dot(hi, rcat_ref[...])
        lg_ref[...] = t[:, :LANES] + t[:, LANES:] + _dot(lo, rcat_ref[:, :LANES])


def _out_proj(oa, ob, oc, x2, w_out, ln_g, router=None):
    n = x2.shape[0]
    tm = ROW_TILE
    with_router = router is not None
    row = lambda w: pl.BlockSpec((tm, w), lambda i: (i, 0))
    const = lambda shape: pl.BlockSpec(shape, lambda i: (0, 0))
    in_specs = [row(NSA_Q_W), row(CONV_CH), row(RET_W), row(D_MODEL), const((D_MODEL, D_MODEL)), const((1, D_MODEL))]
    args = [oa, ob, oc, x2, w_out.astype(BF16), ln_g.reshape(1, D_MODEL)]
    out_specs = [row(D_MODEL), row(D_MODEL)]
    out_shape = [jax.ShapeDtypeStruct((n, D_MODEL), F32), jax.ShapeDtypeStruct((n, D_MODEL), BF16)]
    if with_router:
        rpad = jnp.pad(router, ((0, 0), (0, LANES - N_EXPERTS)))
        rhi = rpad.astype(BF16)
        rlo = (rpad - rhi.astype(F32)).astype(BF16)
        in_specs += [const((D_MODEL, 2 * LANES))]
        args += [jnp.concatenate([rhi, rlo], axis=1)]
        out_specs.append(row(LANES))
        out_shape.append(jax.ShapeDtypeStruct((n, LANES), F32))
    return pl.pallas_call(
        functools.partial(_out_proj_body, with_router=with_router),
        grid=(n // tm,),
        in_specs=in_specs,
        out_specs=out_specs,
        out_shape=out_shape,
        compiler_params=_params("parallel"),
        name="out_proj_router" if with_router else "out_proj",
    )(*args)


def _ffn_body(*refs, with_residual):
    if with_residual:
        e_ref, nv_ref, x_ref, wa_ref, wb_ref, w2_ref, r_ref, o_ref, acc = refs
    else:
        e_ref, nv_ref, x_ref, wa_ref, wb_ref, w2_ref, o_ref, acc = refs
    del e_ref
    f = pl.program_id(1)
    live = pl.program_id(0) < nv_ref[0]

    @pl.when(f == 0)
    def _():
        acc[...] = jnp.zeros_like(acc)

    @pl.when(live)
    def _():
        x = x_ref[...]
        a = _dot(x, wa_ref[...])
        b = _dot(x, wb_ref[...])
        hmid = (a * jax.nn.sigmoid(a) * b).astype(BF16)
        acc[...] += _dot(hmid, w2_ref[...])

    @pl.when(f == pl.num_programs(1) - 1)
    def _():
        o_ref[...] = (acc[...] + r_ref[...] if with_residual else acc[...]).astype(o_ref.dtype)


def _ffn(xs, blk_e, n_live, w13, w2, rows, out_dtype, residual=None):
    p = xs.shape[0]
    nf = D_FF // FF_TILE
    with_residual = residual is not None
    last = lambda i, nv: jnp.minimum(i, nv[0] - 1)
    in_specs = [
        pl.BlockSpec((rows, D_MODEL), lambda i, f, e, nv: (last(i, nv), 0)),
        pl.BlockSpec((None, D_MODEL, FF_TILE), lambda i, f, e, nv: (e[last(i, nv)], 0, f)),
        pl.BlockSpec((None, D_MODEL, FF_TILE), lambda i, f, e, nv: (e[last(i, nv)], 0, nf + f)),
        pl.BlockSpec((None, FF_TILE, D_MODEL), lambda i, f, e, nv: (e[last(i, nv)], f, 0)),
    ]
    args = [xs, w13, w13, w2]
    if with_residual:
        in_specs.append(pl.BlockSpec((rows, D_MODEL), lambda i, f, e, nv: (i, 0)))
        args.append(residual)
    return pl.pallas_call(
        functools.partial(_ffn_body, with_residual=with_residual),
        grid_spec=pltpu.PrefetchScalarGridSpec(
            num_scalar_prefetch=2,
            grid=(p // rows, nf),
            in_specs=in_specs,
            out_specs=pl.BlockSpec((rows, D_MODEL), lambda i, f, e, nv: (i, 0)),
            scratch_shapes=[pltpu.VMEM((rows, D_MODEL), F32)],
        ),
        out_shape=jax.ShapeDtypeStruct((p, D_MODEL), out_dtype),
        compiler_params=_params("parallel", "arbitrary"),
        name="ffn_residual" if with_residual else "ffn_grouped",
    )(blk_e, n_live, *args)


def _final_body(x_ref, y0_ref, y1_ref, gt_ref, g_ref, o_ref):
    gt = gt_ref[...]
    x = x_ref[...] + gt[:, 0:1] * y0_ref[...].astype(F32) + gt[:, 1:2] * y1_ref[...].astype(F32)
    ms = jnp.mean(x * x, axis=-1, keepdims=True)
    o_ref[...] = x * lax.rsqrt(ms + EPS) * g_ref[...]


def _final(x2, y0, y1, gates_pad, ln_g):
    n = x2.shape[0]
    tm = ROW_TILE
    row = lambda w: pl.BlockSpec((tm, w), lambda i: (i, 0))
    return pl.pallas_call(
        _final_body,
        grid=(n // tm,),
        in_specs=[row(D_MODEL), row(D_MODEL), row(D_MODEL), row(LANES), pl.BlockSpec((1, D_MODEL), lambda i: (0, 0))],
        out_specs=row(D_MODEL),
        out_shape=jax.ShapeDtypeStruct((n, D_MODEL), F32),
        compiler_params=_params("parallel"),
        name="moe_combine_final_norm",
    )(x2, y0, y1, gates_pad, ln_g.reshape(1, D_MODEL))


def _overlap_t(seq):
    nc = seq // CMP_STRIDE
    ns = seq // SLC_BLOCK
    cs = np.arange(nc)[None, :] * CMP_STRIDE
    ss = np.arange(ns)[:, None] * SLC_BLOCK
    return jnp.asarray(((cs < ss + SLC_BLOCK) & (cs + CMP_LEN > ss)).astype(np.float32), BF16)


def _route(logits, rows):
    n = logits.shape[0]
    top_logit, top_e = lax.top_k(logits, TOP_K)
    gates = jax.nn.softmax(top_logit, axis=-1)
    e_flat = top_e.reshape(-1).astype(jnp.int32)
    onehot = (e_flat[:, None] == jnp.arange(N_EXPERTS, dtype=jnp.int32)[None, :]).astype(jnp.int32)
    csum = jnp.cumsum(onehot, axis=0)
    counts = csum[-1]
    padded = ((counts + rows - 1) // rows) * rows
    starts = jnp.cumsum(counts) - counts
    pends = jnp.cumsum(padded)
    pstarts = pends - padded
    pos = jnp.sum(onehot * (csum - 1 + pstarts[None, :]), axis=1).reshape(n, TOP_K)
    p = n * TOP_K + N_EXPERTS * rows
    nblk = p // rows
    blk_e = jnp.minimum(jnp.searchsorted(pends, jnp.arange(nblk) * rows, side='right'), N_EXPERTS - 1)
    blk_e = blk_e.astype(jnp.int32)
    order = jnp.argsort(e_flat, stable=True).astype(jnp.int32)
    slot_e = jnp.repeat(blk_e, rows)
    off = jnp.arange(p, dtype=jnp.int32) - pstarts[slot_e]
    live = (off < counts[slot_e]) & (jnp.arange(p) < pends[-1])
    src = jnp.clip(starts[slot_e] + off, 0, n * TOP_K - 1)
    buf_tok = jnp.where(live, order[src] // TOP_K, 0).astype(jnp.int32)
    n_live = (pends[-1:] // rows).astype(jnp.int32)
    return gates, buf_tok, pos, blk_e, n_live


def _token_mixer(x2, l, batch, seq, tables, ovt, ln_attn, w_in, cmp_pe, cmp_w1, cmp_w2, conv_w, conv_b, conv_ln_g,
                 conv_ln_b, conv_pw, ret_gn):
    hm, ck, wd, gates, conv_in, ret = _norm_proj(x2, ln_attn[l], _arrange_w_in(w_in[l]), seq)
    nsub = seq // CMP_STRIDE
    half = CMP_STRIDE * HEAD_DIM
    sub = ck.reshape(CK_SLABS, batch, nsub, half)
    w1 = cmp_w1[l]
    w1cat = jnp.concatenate([w1[:, :half], w1[:, half:]], axis=2).astype(BF16)
    pe8 = jnp.broadcast_to(cmp_pe[l].reshape(2, 1, CMP_LEN * HEAD_DIM), (2, 8, CMP_LEN * HEAD_DIM)).astype(BF16)
    kvc = _compress(sub, pe8, w1cat, cmp_w2[l].astype(BF16))
    o_a = _nsa(hm, wd, gates, kvc, ovt, batch, seq)
    w_taps = jnp.pad(conv_w[l], ((0, CONV_TAP_ROWS - CONV_K), (0, 0)))
    o_b = _conv(conv_in, w_taps, conv_b[l], conv_ln_g[l], conv_ln_b[l], conv_pw[l].astype(BF16), batch, seq)
    o_c = _retention(ret, ret_gn[l], tables, batch, seq)
    return o_a, o_b, o_c


def kernel(x, ln_attn, w_in, cmp_pe, cmp_w1, cmp_w2, conv_w, conv_b, conv_ln_g, conv_ln_b, conv_pw, ret_gn, w_out,
           ln_ffn, ffn_w13, ffn_w2, router, moe_w13, moe_w2, ln_final):
    batch, seq, d = x.shape
    n = batch * seq
    depth = ln_attn.shape[0]
    x2 = x.reshape(n, d)
    tables = _retention_tables(seq)
    ovt = _overlap_t(seq)
    out = None
    for l in range(depth):
        o_a, o_b, o_c = _token_mixer(x2, l, batch, seq, tables, ovt, ln_attn, w_in, cmp_pe, cmp_w1, cmp_w2, conv_w,
                                     conv_b, conv_ln_g, conv_ln_b, conv_pw, ret_gn)
        if l % 2 == 0:
            x2, h2 = _out_proj(o_a, o_b, o_c, x2, w_out[l], ln_ffn[l])
            nblk = n // FFN_ROWS
            x2 = _ffn(h2, jnp.zeros((nblk,), jnp.int32), jnp.full((1,), nblk, jnp.int32),
                      ffn_w13[l // 2][None].astype(BF16), ffn_w2[l // 2][None].astype(BF16), FFN_ROWS, F32, residual=x2)
        else:
            x2, h2, logits = _out_proj(o_a, o_b, o_c, x2, w_out[l], ln_ffn[l], router=router[l // 2])
            gates, buf_tok, pos, blk_e, n_live = _route(logits[:, :N_EXPERTS], MOE_ROWS)
            xs = h2[buf_tok]
            yb = _ffn(xs, blk_e, n_live, moe_w13[l // 2].astype(BF16), moe_w2[l // 2].astype(BF16), MOE_ROWS, BF16)
            gates_pad = jnp.pad(gates, ((0, 0), (0, LANES - TOP_K)))
            if l == depth - 1:
                out = _final(x2, yb[pos[:, 0]], yb[pos[:, 1]], gates_pad, ln_final)
            else:
                x2 = x2 + gates[:, 0:1] * yb[pos[:, 0]] + gates[:, 1:2] * yb[pos[:, 1]]
    if out is None:
        out = _final(x2, jnp.zeros_like(x2), jnp.zeros_like(x2), jnp.zeros((n, LANES), F32), ln_final)
    return out.reshape(batch, seq, d)
```

```python
import functools

import numpy as np
import jax
import jax.numpy as jnp
from jax import lax
from jax.experimental import pallas as pl
from jax.experimental.pallas import tpu as pltpu

D_MODEL = 1024
HEAD_DIM = 64
NSA_HEADS = 8
NSA_KV_HEADS = 2
NSA_GROUP = NSA_HEADS // NSA_KV_HEADS
CMP_LEN = 32
CMP_STRIDE = 16
CMP_HIDDEN = 256
SLC_BLOCK = 64
SLC_TOPK = 16
WINDOW = 512
N_BRANCH = 3
CONV_CH = 256
CONV_K = 31
RET_HEADS = 4
RET_CHUNK = 128
ROPE_BASE = 10000.0
D_FF = 2816
N_EXPERTS = 8
TOP_K = 2
NEG_INF = -1e30
FORCE_SCORE = 1e9
EPS = 1e-6

NSA_Q_W = NSA_HEADS * HEAD_DIM
NSA_KV_W = NSA_KV_HEADS * HEAD_DIM
NSA_GATE_W = NSA_HEADS * N_BRANCH
RET_W = RET_HEADS * HEAD_DIM

LANES = 128
SUBLANES = 8
Q_TILE = 512
WIN_TILE = 256
ROW_BLK = 32
SEL_CHUNK = 512
ROW_TILE = 512
RET_BATCH = 8
CONV_TILE = 512
CONV_HALO = 32
CONV_TAP_ROWS = 32
MOE_ROWS = 512
FFN_ROWS = 512
FF_TILE = 1408
V7X_VMEM_BYTES = 64 * 1024 * 1024
VMEM_LIMIT = V7X_VMEM_BYTES * 7 // 8

BF16 = jnp.bfloat16
F32 = jnp.float32

HM_Q, HM_KW, HM_SLABS = 0, 8, 10
CK_SLABS = 4
WD_KS, WD_VS, WD_VW, WD_SLABS = 0, 2, 4, 6
LOG2E = 1.4426950408889634


def _dot(a, b):
    return jnp.dot(a, b, preferred_element_type=F32)


def _dot_nt(a, b):
    return lax.dot_general(a, b, (((1,), (1,)), ((), ())), preferred_element_type=F32)


def _split3(x):
    hi = x.astype(BF16)
    r1 = x - hi.astype(F32)
    mid = r1.astype(BF16)
    lo = (r1 - mid.astype(F32)).astype(BF16)
    return hi, mid, lo


def _params(*sem):
    return pltpu.CompilerParams(dimension_semantics=sem, vmem_limit_bytes=VMEM_LIMIT)


W_HM = (HM_SLABS + CK_SLABS) * HEAD_DIM
W_WD = W_HM + 3 * NSA_KV_W
W_GATE = W_WD + 2 * LANES
W_CONV = W_GATE + 2 * CONV_CH
W_RET = W_CONV + 4 * RET_W


def _arrange_w_in(w_in):
    offs = np.cumsum([0, NSA_Q_W] + [NSA_KV_W] * 6 + [NSA_GATE_W, 2 * CONV_CH] + [RET_W] * 4)
    q, kc, vc, ks, vs, kw, vw, gl, conv, rq, rk, rv, rg = [w_in[:, offs[i]:offs[i + 1]] for i in range(13)]
    scale = HEAD_DIM ** -0.5
    gpad = jnp.zeros((D_MODEL, LANES - NSA_GATE_W // 2), w_in.dtype)
    gates = jnp.concatenate([gl[:, :NSA_GATE_W // 2], gpad, gl[:, NSA_GATE_W // 2:], gpad], axis=1)
    w = jnp.concatenate([q * (scale * LOG2E), kw, kc, vc, ks, vs, vw, gates, conv, rq, rk * scale, rv, rg], axis=1)
    return w.astype(BF16)


def _norm_proj_body(x_ref, g_ref, w_ref, hm_ref, ck_ref, wd_ref, gates_ref, conv_ref, ret_ref, ck_scr, *, seq):
    x = x_ref[...]
    ms = jnp.mean(x * x, axis=-1, keepdims=True)
    h = (x * lax.rsqrt(ms + EPS) * g_ref[...]).astype(BF16)
    res = _dot(h, w_ref[:, 0:W_HM])
    for j in range(HM_SLABS):
        hm_ref[j] = res[:, j * HEAD_DIM:(j + 1) * HEAD_DIM].astype(BF16)
    nrow = x.shape[0] // CMP_STRIDE
    for k in range(ck_scr.shape[0]):
        c0 = HM_SLABS * HEAD_DIM + k * LANES
        ck_scr[k] = res[:, c0:c0 + LANES]
        toks = [ck_scr[k, pl.ds(t, nrow, stride=CMP_STRIDE), :] for t in range(CMP_STRIDE)]
        for half in range(LANES // HEAD_DIM):
            cols = slice(half * HEAD_DIM, (half + 1) * HEAD_DIM)
            ck_ref[k * (LANES // HEAD_DIM) + half] = jnp.concatenate([tk[:, cols] for tk in toks], axis=1).astype(BF16)
    rows = x.shape[0]
    t0 = (pl.program_id(0) % (seq // rows)) * rows
    lane = lax.broadcasted_iota(jnp.int32, (rows, HEAD_DIM), 1)
    blk = (t0 + lax.broadcasted_iota(jnp.int32, (rows, HEAD_DIM), 0)) // SLC_BLOCK
    onehot = jnp.where(blk == lane, 1.0, 0.0)
    ones_col = jnp.where(lane == 0, 1.0, 0.0)
    wide = _dot(h, w_ref[:, W_HM:W_WD])
    for j, ext in ((WD_KS, onehot), (WD_VS, ones_col), (WD_VW, ones_col)):
        for g in range(NSA_KV_HEADS):
            c = (j + g) * HEAD_DIM
            wd_ref[j + g] = jnp.concatenate([wide[:, c:c + HEAD_DIM], ext], axis=1).astype(BF16)
    gates_ref[...] = _dot(h, w_ref[:, W_WD:W_GATE])
    conv_ref[...] = _dot(h, w_ref[:, W_GATE:W_CONV]).astype(BF16)
    ret_ref[...] = _dot(h, w_ref[:, W_CONV:W_RET]).astype(BF16)


def _norm_proj(x2, ln_g, w_arr, seq):
    n = x2.shape[0]
    tm = ROW_TILE
    return pl.pallas_call(
        functools.partial(_norm_proj_body, seq=seq),
        grid=(n // tm,),
        in_specs=[
            pl.BlockSpec((tm, D_MODEL), lambda i: (i, 0)),
            pl.BlockSpec((1, D_MODEL), lambda i: (0, 0)),
            pl.BlockSpec((D_MODEL, W_RET), lambda i: (0, 0)),
        ],
        out_specs=[
            pl.BlockSpec((HM_SLABS, tm, HEAD_DIM), lambda i: (0, i, 0)),
            pl.BlockSpec((CK_SLABS, tm // CMP_STRIDE, CMP_STRIDE * HEAD_DIM), lambda i: (0, i, 0)),
            pl.BlockSpec((WD_SLABS, tm, LANES), lambda i: (0, i, 0)),
            pl.BlockSpec((tm, 2 * LANES), lambda i: (i, 0)),
            pl.BlockSpec((tm, 2 * CONV_CH), lambda i: (i, 0)),
            pl.BlockSpec((tm, 4 * RET_W), lambda i: (i, 0)),
        ],
        out_shape=[
            jax.ShapeDtypeStruct((HM_SLABS, n, HEAD_DIM), BF16),
            jax.ShapeDtypeStruct((CK_SLABS, n // CMP_STRIDE, CMP_STRIDE * HEAD_DIM), BF16),
            jax.ShapeDtypeStruct((WD_SLABS, n, LANES), BF16),
            jax.ShapeDtypeStruct((n, 2 * LANES), F32),
            jax.ShapeDtypeStruct((n, 2 * CONV_CH), BF16),
            jax.ShapeDtypeStruct((n, 4 * RET_W), BF16),
        ],
        scratch_shapes=[pltpu.VMEM((CK_SLABS * HEAD_DIM // LANES, tm, LANES), F32)],
        compiler_params=_params("parallel"),
        name="norm_proj",
    )(x2, ln_g.reshape(1, D_MODEL), w_arr)


def _compress_body(s_ref, pe_ref, w1_ref, w2_ref, o_ref):
    u = _dot(s_ref[...], w1_ref[...])
    pe = pe_ref[...]
    bias = (_dot(pe[:, :CMP_STRIDE * HEAD_DIM], w1_ref[:, :CMP_HIDDEN])
            + _dot(pe[:, CMP_STRIDE * HEAD_DIM:], w1_ref[:, CMP_HIDDEN:]))[0:1]
    bot = u[:, CMP_HIDDEN:]
    bot_next = jnp.concatenate([bot[1:], bot[:1]], axis=0)
    hid = u[:, :CMP_HIDDEN] + bot_next + bias
    hid = hid * jax.nn.sigmoid(hid)
    o_ref[...] = _dot(hid.astype(BF16), w2_ref[...]).astype(BF16)


def _compress(sub, pe8, w1cat, w2):
    _, b, nsub, width = sub.shape
    return pl.pallas_call(
        _compress_body,
        grid=(4, b),
        in_specs=[
            pl.BlockSpec((None, None, nsub, width), lambda j, i: (j, i, 0, 0)),
            pl.BlockSpec((None, 8, 2 * width), lambda j, i: (j // 2, 0, 0)),
            pl.BlockSpec((None, width, 2 * CMP_HIDDEN), lambda j, i: (j // 2, 0, 0)),
            pl.BlockSpec((None, CMP_HIDDEN, HEAD_DIM), lambda j, i: (j // 2, 0, 0)),
        ],
        out_specs=pl.BlockSpec((None, None, nsub, HEAD_DIM), lambda j, i: (j, i, 0, 0)),
        out_shape=jax.ShapeDtypeStruct((4, b, nsub, HEAD_DIM), BF16),
        compiler_params=_params("parallel", "parallel"),
        name="compress",
    )(sub, pe8, w1cat, w2)


def _topk_block_bias(imp_t):
    ns, nq = imp_t.shape
    sub = SUBLANES
    nb = ns // sub
    blocks = [imp_t[v * sub:(v + 1) * sub] for v in range(nb)]
    cnts = [jnp.zeros((sub, nq), jnp.int32) for _ in range(nb)]
    sub_iota = lax.broadcasted_iota(jnp.int32, (sub, nq), 0)
    for i in range(ns):
        row = jnp.broadcast_to(imp_t[i:i + 1], (sub, nq))
        for v in range(nb):
            if v > i // sub:
                cnts[v] = cnts[v] + jnp.where(row >= blocks[v], 1, 0)
            elif v < i // sub:
                cnts[v] = cnts[v] + jnp.where(row > blocks[v], 1, 0)
            else:
                tie = (row == blocks[v]) & (sub_iota > i % sub)
                cnts[v] = cnts[v] + jnp.where(row > blocks[v], 1, 0) + jnp.where(tie, 1, 0)
    cnt = jnp.concatenate(cnts, axis=0)
    return jnp.where(cnt < SLC_TOPK, 0.0, NEG_INF)


def _flash_update(qs, k, v1, bias, state):
    nh = len(state)
    qn = qs.shape[0] // nh
    s_all = _dot_nt(qs, k)
    ps, ms, alphas = [], [], []
    for h in range(nh):
        s = s_all[h * qn:(h + 1) * qn]
        if bias is not None:
            s = s + bias
        m = state[h][0]
        m_new = jnp.maximum(m, jnp.max(s, axis=-1, keepdims=True))
        ps.append(jnp.exp2(s - m_new).astype(BF16))
        alphas.append(jnp.exp2(m - m_new))
        ms.append(m_new)
    pv = _dot(jnp.concatenate(ps, axis=0), v1)
    return tuple((ms[h], alphas[h] * state[h][1] + pv[h * qn:(h + 1) * qn]) for h in range(nh))


def _flash_update_staged(qs, k, v1, bias, nh, s_scr, p_scr, m_scr, al_scr, acc_scr):
    rows, width = qs.shape[0], k.shape[0]
    qn = rows // nh
    s_scr[...] = _dot_nt(qs, k)
    for b in range(rows // ROW_BLK):
        r = slice(b * ROW_BLK, (b + 1) * ROW_BLK)
        q0 = (b * ROW_BLK) % qn
        s = s_scr[r] + bias[q0:q0 + ROW_BLK]
        m_old = m_scr[r]
        m_new = jnp.maximum(m_old, jnp.broadcast_to(jnp.max(s, axis=-1, keepdims=True), (ROW_BLK, LANES)))
        al_scr[r] = jnp.exp2(m_old - m_new)
        m_scr[r] = m_new
        p_scr[r] = jnp.exp2(s - jnp.concatenate([m_new] * (width // LANES), axis=1)).astype(BF16)
    acc_scr[...] = al_scr[...] * acc_scr[...] + _dot(p_scr[...], v1)


def _nsa_body(q_ref, gl_ref, kc_ref, vc_ref, ks_ref, vs_ref, kw_ref, vw_ref, ovt_ref, o_ref,
              s_scr, p_scr, m_sel, al_sel, acc_sel):
    i = pl.program_id(2)
    hpg = NSA_GROUP
    q = q_ref[...].reshape(hpg * Q_TILE, HEAD_DIM)
    rr = lax.broadcasted_iota(jnp.int32, (Q_TILE, 1), 0)
    t_q = i * Q_TILE + rr
    heads = [slice(h * Q_TILE, (h + 1) * Q_TILE) for h in range(hpg)]

    kc = kc_ref[...]
    ncb = kc.shape[0]
    s_c = _dot_nt(q, kc)
    c_end = lax.broadcasted_iota(jnp.int32, (1, ncb), 1) * CMP_STRIDE + (CMP_LEN - 1)
    cbias = jnp.where(c_end <= t_q, 0.0, NEG_INF)
    some_block = t_q >= CMP_LEN - 1
    pcs, psum = [], None
    for r in heads:
        s = s_c[r] + cbias
        p = jnp.exp2(s - jnp.max(s, axis=-1, keepdims=True))
        l = jnp.sum(p, axis=-1, keepdims=True)
        p = p * jnp.where(some_block, 1.0 / l, 0.0)
        pcs.append(p.astype(BF16))
        psum = p if psum is None else psum + p
    o_cmp = _dot(jnp.concatenate(pcs, axis=0), vc_ref[...])

    ovt = ovt_ref[...]
    imp_t = sum(_dot_nt(ovt, part) for part in _split3(psum))
    ns = imp_t.shape[0]
    sblk = lax.broadcasted_iota(jnp.int32, (ns, Q_TILE), 0)
    cur = (i * Q_TILE + lax.broadcasted_iota(jnp.int32, (ns, Q_TILE), 1)) // SLC_BLOCK
    forced = (sblk == 0) | (sblk == cur) | (sblk == cur - 1)
    imp_t = jnp.where(sblk <= cur, jnp.where(forced, FORCE_SCORE, imp_t), NEG_INF)
    few_blocks = (i + 1) * Q_TILE <= SLC_TOPK * SLC_BLOCK
    bias_t = lax.cond(few_blocks, lambda x: jnp.where(x > 0.5 * NEG_INF, 0.0, NEG_INF), _topk_block_bias, imp_t)
    bias_pad = jnp.concatenate([bias_t, jnp.zeros((LANES - ns, Q_TILE), F32)], axis=0).T
    bias = bias_pad[:, :ns].astype(BF16)
    qp = jnp.concatenate([q, jnp.concatenate([bias] * hpg, axis=0)], axis=1)

    m_sel[...] = jnp.full(m_sel.shape, NEG_INF, F32)
    acc_sel[...] = jnp.zeros(acc_sel.shape, F32)
    n_full = (i * Q_TILE) // SEL_CHUNK

    def sel_step(c, carry):
        k0 = pl.multiple_of(c * SEL_CHUNK, SEL_CHUNK)
        kpos = k0 + lax.broadcasted_iota(jnp.int32, (1, SEL_CHUNK), 1)
        causal = jnp.where(kpos <= t_q, 0.0, NEG_INF)
        _flash_update_staged(qp, ks_ref[pl.ds(k0, SEL_CHUNK), :], vs_ref[pl.ds(k0, SEL_CHUNK), :], causal, hpg,
                             s_scr, p_scr, m_sel, al_sel, acc_sel)
        return carry

    lax.fori_loop(0, n_full + 1, sel_step, 0)

    wkeys = WINDOW + WIN_TILE
    init_w = tuple((jnp.full((WIN_TILE, 1), NEG_INF, F32), jnp.zeros((WIN_TILE, LANES), F32)) for _ in range(hpg))
    st_w = []
    for u in range(Q_TILE // WIN_TILE):
        sub = slice(u * WIN_TILE, (u + 1) * WIN_TILE)
        q_u = jnp.concatenate([q[r][sub] for r in heads], axis=0)
        w0 = pl.multiple_of(jnp.maximum(i * Q_TILE + u * WIN_TILE - WINDOW, 0), WIN_TILE)
        dpos = t_q[sub] - (w0 + lax.broadcasted_iota(jnp.int32, (1, wkeys), 1))
        band = jnp.where((dpos >= 0) & (dpos < WINDOW), 0.0, NEG_INF)
        st_w.append(_flash_update(q_u, kw_ref[pl.ds(w0, wkeys), :], vw_ref[pl.ds(w0, wkeys), :], band, init_w))

    sig = jax.nn.sigmoid(gl_ref[...])
    outs = []
    for h, r in enumerate(heads):
        c = h * N_BRANCH
        acc_s = acc_sel[r]
        acc_w = jnp.concatenate([st_u[h][1] for st_u in st_w], axis=0)
        g_s = sig[:, c + 1:c + 2] / acc_s[:, HEAD_DIM:HEAD_DIM + 1]
        g_w = sig[:, c + 2:c + 3] / acc_w[:, HEAD_DIM:HEAD_DIM + 1]
        outs.append(sig[:, c:c + 1] * o_cmp[r] + g_s * acc_s[:, :HEAD_DIM] + g_w * acc_w[:, :HEAD_DIM])
    o_ref[...] = jnp.concatenate(outs, axis=1).astype(BF16)


def _nsa(hm, wd, gates, kvc, ovt, batch, seq):
    n = batch * seq
    nq = seq // Q_TILE
    ncb = kvc.shape[2]
    g_ = NSA_KV_HEADS
    rows = NSA_GROUP * Q_TILE
    return pl.pallas_call(
        _nsa_body,
        grid=(batch, g_, nq),
        in_specs=[
            pl.BlockSpec((NSA_GROUP, Q_TILE, HEAD_DIM), lambda b, g, i: (g, b * nq + i, 0)),
            pl.BlockSpec((Q_TILE, LANES), lambda b, g, i: (b * nq + i, g)),
            pl.BlockSpec((None, None, ncb, HEAD_DIM), lambda b, g, i: (g, b, 0, 0)),
            pl.BlockSpec((None, None, ncb, HEAD_DIM), lambda b, g, i: (g_ + g, b, 0, 0)),
            pl.BlockSpec((None, seq, LANES), lambda b, g, i: (WD_KS + g, b, 0)),
            pl.BlockSpec((None, seq, LANES), lambda b, g, i: (WD_VS + g, b, 0)),
            pl.BlockSpec((None, seq, HEAD_DIM), lambda b, g, i: (HM_KW + g, b, 0)),
            pl.BlockSpec((None, seq, LANES), lambda b, g, i: (WD_VW + g, b, 0)),
            pl.BlockSpec(ovt.shape, lambda b, g, i: (0, 0)),
        ],
        out_specs=pl.BlockSpec((Q_TILE, NSA_GROUP * HEAD_DIM), lambda b, g, i: (b * nq + i, g)),
        out_shape=jax.ShapeDtypeStruct((n, NSA_Q_W), BF16),
        scratch_shapes=[
            pltpu.VMEM((rows, SEL_CHUNK), F32), pltpu.VMEM((rows, SEL_CHUNK), BF16),
            pltpu.VMEM((rows, LANES), F32), pltpu.VMEM((rows, LANES), F32), pltpu.VMEM((rows, LANES), F32),
        ],
        compiler_params=_params("parallel", "parallel", "arbitrary"),
        name="nsa",
    )(hm, gates, kvc, kvc, wd, wd, hm, wd, ovt)


def _conv_body(u_ref, up_ref, w_ref, b_ref, lg_ref, lb_ref, pw_ref, o_ref, hbuf, sbuf):
    i = pl.program_id(1)

    def glu(u):
        u = u.astype(F32)
        return u[:, :CONV_CH] * jax.nn.sigmoid(u[:, CONV_CH:])

    prev = glu(up_ref[...])
    hbuf[0:CONV_HALO] = jnp.where(i > 0, prev, 0.0)
    hbuf[CONV_HALO:] = glu(u_ref[...])
    span = sbuf.shape[1]
    for s in range(1, SUBLANES):
        sbuf[s - 1] = hbuf[s:s + span]
    w = w_ref[...]
    acc = jnp.zeros((CONV_TILE, CONV_CH), F32) + b_ref[...]
    base = CONV_HALO - (CONV_K - 1)
    for j in range(CONV_K):
        s = (base + j) % SUBLANES
        a8 = base + j - s
        src = hbuf[a8:a8 + CONV_TILE] if s == 0 else sbuf[s - 1, a8:a8 + CONV_TILE]
        acc = acc + w[j:j + 1] * src
    mu = jnp.mean(acc, axis=-1, keepdims=True)
    var = jnp.mean(jnp.square(acc - mu), axis=-1, keepdims=True)
    hf = (acc - mu) * lax.rsqrt(var + 1e-5) * lg_ref[...] + lb_ref[...]
    hf = hf * jax.nn.sigmoid(hf)
    o_ref[...] = _dot(hf.astype(BF16), pw_ref[...]).astype(BF16)


def _conv(conv_in, w_taps, b, lg, lb, pw, batch, seq):
    n = batch * seq
    nt = seq // CONV_TILE
    halo_per_tile = CONV_TILE // CONV_HALO
    row = lambda v: v.reshape(1, CONV_CH)
    return pl.pallas_call(
        _conv_body,
        grid=(batch, nt),
        in_specs=[
            pl.BlockSpec((CONV_TILE, 2 * CONV_CH), lambda b_, i: (b_ * nt + i, 0)),
            pl.BlockSpec((CONV_HALO, 2 * CONV_CH),
                         lambda b_, i: (jnp.maximum((b_ * nt + i) * halo_per_tile - 1, 0), 0)),
            pl.BlockSpec((CONV_TAP_ROWS, CONV_CH), lambda b_, i: (0, 0)),
            pl.BlockSpec((1, CONV_CH), lambda b_, i: (0, 0)),
            pl.BlockSpec((1, CONV_CH), lambda b_, i: (0, 0)),
            pl.BlockSpec((1, CONV_CH), lambda b_, i: (0, 0)),
            pl.BlockSpec((CONV_CH, CONV_CH), lambda b_, i: (0, 0)),
        ],
        out_specs=pl.BlockSpec((CONV_TILE, CONV_CH), lambda b_, i: (b_ * nt + i, 0)),
        out_shape=jax.ShapeDtypeStruct((n, CONV_CH), BF16),
        scratch_shapes=[pltpu.VMEM((CONV_HALO + CONV_TILE, CONV_CH), F32),
                        pltpu.VMEM((SUBLANES - 1, CONV_HALO + CONV_TILE - SUBLANES, CONV_CH), F32)],
        compiler_params=_params("parallel", "parallel"),
        name="conformer_conv",
    )(conv_in, conv_in, w_taps, row(b), row(lg), row(lb), pw)


def _ret_body(q_ref, k_ref, v_ref, g_ref, cos_ref, sin_ref, perm_ref, decay_ref, xi_ref, zeta_ref, maskk_ref,
              maskv_ref, gmat_ref, masks_ref, mavg_ref, gn_ref, o_ref, state):
    c = pl.program_id(1)

    @pl.when(c == 0)
    def _():
        state[...] = jnp.zeros_like(state)

    nb = q_ref.shape[0]
    chunk = q_ref.shape[1]
    cos, sin = cos_ref[...], sin_ref[...]
    qk = jnp.concatenate([q_ref[bi] for bi in range(nb)] + [k_ref[bi] for bi in range(nb)], axis=0)
    rot = _dot(qk, perm_ref[...])
    outs, new_states = [], []
    for bi in range(nb):
        qs, ks = slice(bi * chunk, (bi + 1) * chunk), slice((nb + bi) * chunk, (nb + bi + 1) * chunk)
        qb = (q_ref[bi].astype(F32) * cos + rot[qs] * sin).astype(BF16)
        kr_t = (k_ref[bi].astype(F32) * cos + rot[ks] * sin).T
        k_bd = jnp.concatenate([kr_t.astype(BF16)] * RET_HEADS, axis=1) * maskk_ref[...]
        a = _dot(qb, k_bd) * decay_ref[...]
        v = v_ref[bi]
        v_bd = jnp.concatenate([v] * RET_HEADS, axis=0) * maskv_ref[...]
        s_bd = state[bi]
        outs.append(_dot(a.astype(BF16), v_bd) + _dot(qb, s_bd.astype(BF16)) * xi_ref[...])
        kz_t = (kr_t * zeta_ref[...]).astype(BF16)
        new_states.append(gmat_ref[...] * s_bd + _dot(kz_t, v) * masks_ref[...])
    o = jnp.concatenate(outs, axis=0)

    def head_mean(x):
        hi = x.astype(BF16)
        lo = (x - hi.astype(F32)).astype(BF16)
        return _dot(hi, mavg_ref[...]) + _dot(lo, mavg_ref[...])

    d = o - head_mean(o)
    y = d * lax.rsqrt(head_mean(d * d) + 1e-5) * gn_ref[...]
    for bi in range(nb):
        gate = g_ref[bi].astype(F32)
        o_ref[bi] = (gate * jax.nn.sigmoid(gate) * y[bi * chunk:(bi + 1) * chunk]).astype(BF16)
        state[bi] = new_states[bi]


def _retention_tables(seq):
    half = HEAD_DIM // 2
    pos = jnp.arange(seq, dtype=F32)
    inv = ROPE_BASE ** (-jnp.arange(half, dtype=F32) / half)
    ang = pos[:, None] * inv[None, :]
    cos, sin = jnp.cos(ang), jnp.sin(ang)
    cos_t = jnp.tile(jnp.concatenate([cos, cos], axis=1), (1, RET_HEADS))
    sin_t = jnp.tile(jnp.concatenate([-sin, sin], axis=1), (1, RET_HEADS))
    log_gamma = jnp.log1p(-jnp.exp2(-5.0 - jnp.arange(RET_HEADS, dtype=F32)))
    i = jnp.arange(RET_CHUNK, dtype=F32)
    rel = i[:, None] - i[None, :]
    decay = jnp.where(rel >= 0, jnp.exp(jnp.maximum(rel, 0.0)[None] * log_gamma[:, None, None]), 0.0)
    decay_cat = decay.transpose(1, 0, 2).reshape(RET_CHUNK, RET_HEADS * RET_CHUNK)
    xi = jnp.exp((i + 1.0)[None, :] * log_gamma[:, None])
    zeta = jnp.exp((RET_CHUNK - 1.0 - i)[None, :] * log_gamma[:, None])
    g_chunk = jnp.exp(RET_CHUNK * log_gamma)
    lanes = lambda a: jnp.repeat(a.T, HEAD_DIM, axis=1)
    j = np.arange(RET_W)
    partner = (j // HEAD_DIM) * HEAD_DIM + (j % HEAD_DIM + half) % HEAD_DIM
    perm = np.zeros((RET_W, RET_W), np.float32)
    perm[partner, j] = 1.0
    head_of_feat = j // HEAD_DIM
    head_of_key = np.arange(RET_HEADS * RET_CHUNK) // RET_CHUNK
    maskk = (head_of_feat[:, None] == head_of_key[None, :]).astype(np.float32)
    masks = (head_of_feat[:, None] == head_of_feat[None, :]).astype(np.float32)
    gmat = jnp.asarray(masks) * jnp.repeat(g_chunk, HEAD_DIM)[:, None]
    return (cos_t, sin_t, jnp.asarray(perm, BF16), decay_cat, lanes(xi), lanes(zeta).T, jnp.asarray(maskk, BF16),
            jnp.asarray(maskk.T, BF16), gmat, jnp.asarray(masks), jnp.asarray(masks / HEAD_DIM, BF16))


def _retention(ret, gn_g, tables, batch, seq):
    nc = seq // RET_CHUNK
    rb = RET_BATCH
    cos_t, sin_t = tables[:2]
    consts = tables[2:]
    ret3 = ret.reshape(batch, seq, 4 * RET_W)
    tile = lambda col: pl.BlockSpec((rb, RET_CHUNK, RET_W), lambda b, c: (b, c, col))
    const = lambda a: pl.BlockSpec(a.shape, lambda b, c: (0,) * a.ndim)
    out = pl.pallas_call(
        _ret_body,
        grid=(batch // rb, nc),
        in_specs=[
            tile(0), tile(1), tile(2), tile(3),
            pl.BlockSpec((RET_CHUNK, RET_W), lambda b, c: (c, 0)),
            pl.BlockSpec((RET_CHUNK, RET_W), lambda b, c: (c, 0)),
            *[const(a) for a in consts],
            pl.BlockSpec((1, RET_W), lambda b, c: (0, 0)),
        ],
        out_specs=pl.BlockSpec((rb, RET_CHUNK, RET_W), lambda b, c: (b, c, 0)),
        out_shape=jax.ShapeDtypeStruct((batch, seq, RET_W), BF16),
        scratch_shapes=[pltpu.VMEM((rb, RET_W, RET_W), F32)],
        compiler_params=_params("parallel", "arbitrary"),
        name="retention",
    )(ret3, ret3, ret3, ret3, cos_t, sin_t, *consts, gn_g.reshape(1, RET_W))
    return out.reshape(batch * seq, RET_W)


def _out_proj_body(*refs, with_router):
    if with_router:
        oa_ref, ob_ref, oc_ref, x_ref, w_ref, g_ref, rcat_ref, xo_ref, h_ref, lg_ref = refs
    else:
        oa_ref, ob_ref, oc_ref, x_ref, w_ref, g_ref, xo_ref, h_ref = refs
    y = (_dot(oa_ref[...], w_ref[0:NSA_Q_W])
         + _dot(ob_ref[...], w_ref[NSA_Q_W:NSA_Q_W + CONV_CH])
         + _dot(oc_ref[...], w_ref[NSA_Q_W + CONV_CH:]))
    x = x_ref[...] + y
    xo_ref[...] = x
    ms = jnp.mean(x * x, axis=-1, keepdims=True)
    h = x * lax.rsqrt(ms + EPS) * g_ref[...]
    hi = h.astype(BF16)
    h_ref[...] = hi
    if with_router:
        lo = (h - hi.astype(F32)).astype(BF16)
        t = _dot(hi, rcat_ref[...])
        lg_ref[...] = t[:, :LANES] + t[:, LANES:] + _dot(lo, rcat_ref[:, :LANES])


def _out_proj(oa, ob, oc, x2, w_out, ln_g, router=None):
    n = x2.shape[0]
    tm = ROW_TILE
    with_router = router is not None
    row = lambda w: pl.BlockSpec((tm, w), lambda i: (i, 0))
    const = lambda shape: pl.BlockSpec(shape, lambda i: (0, 0))
    in_specs = [row(NSA_Q_W), row(CONV_CH), row(RET_W), row(D_MODEL), const((D_MODEL, D_MODEL)), const((1, D_MODEL))]
    args = [oa, ob, oc, x2, w_out.astype(BF16), ln_g.reshape(1, D_MODEL)]
    out_specs = [row(D_MODEL), row(D_MODEL)]
    out_shape = [jax.ShapeDtypeStruct((n, D_MODEL), F32), jax.ShapeDtypeStruct((n, D_MODEL), BF16)]
    if with_router:
        rpad = jnp.pad(router, ((0, 0), (0, LANES - N_EXPERTS)))
        rhi = rpad.astype(BF16)
        rlo = (rpad - rhi.astype(F32)).astype(BF16)
        in_specs += [const((D_MODEL, 2 * LANES))]
        args += [jnp.concatenate([rhi, rlo], axis=1)]
        out_specs.append(row(LANES))
        out_shape.append(jax.ShapeDtypeStruct((n, LANES), F32))
    return pl.pallas_call(
        functools.partial(_out_proj_body, with_router=with_router),
        grid=(n // tm,),
        in_specs=in_specs,
        out_specs=out_specs,
        out_shape=out_shape,
        compiler_params=_params("parallel"),
        name="out_proj_router" if with_router else "out_proj",
    )(*args)


def _ffn_body(*refs, with_residual):
    if with_residual:
        e_ref, nv_ref, x_ref, wa_ref, wb_ref, w2_ref, r_ref, o_ref, acc = refs
    else:
        e_ref, nv_ref, x_ref, wa_ref, wb_ref, w2_ref, o_ref, acc = refs
    del e_ref
    f = pl.program_id(1)
    live = pl.program_id(0) < nv_ref[0]

    @pl.when(f == 0)
    def _():
        acc[...] = jnp.zeros_like(acc)

    @pl.when(live)
    def _():
        x = x_ref[...]
        a = _dot(x, wa_ref[...])
        b = _dot(x, wb_ref[...])
        hmid = (a * jax.nn.sigmoid(a) * b).astype(BF16)
        acc[...] += _dot(hmid, w2_ref[...])

    @pl.when(f == pl.num_programs(1) - 1)
    def _():
        o_ref[...] = (acc[...] + r_ref[...] if with_residual else acc[...]).astype(o_ref.dtype)


def _ffn(xs, blk_e, n_live, w13, w2, rows, out_dtype, residual=None):
    p = xs.shape[0]
    nf = D_FF // FF_TILE
    with_residual = residual is not None
    last = lambda i, nv: jnp.minimum(i, nv[0] - 1)
    in_specs = [
        pl.BlockSpec((rows, D_MODEL), lambda i, f, e, nv: (last(i, nv), 0)),
        pl.BlockSpec((None, D_MODEL, FF_TILE), lambda i, f, e, nv: (e[last(i, nv)], 0, f)),
        pl.BlockSpec((None, D_MODEL, FF_TILE), lambda i, f, e, nv: (e[last(i, nv)], 0, nf + f)),
        pl.BlockSpec((None, FF_TILE, D_MODEL), lambda i, f, e, nv: (e[last(i, nv)], f, 0)),
    ]
    args = [xs, w13, w13, w2]
    if with_residual:
        in_specs.append(pl.BlockSpec((rows, D_MODEL), lambda i, f, e, nv: (i, 0)))
        args.append(residual)
    return pl.pallas_call(
        functools.partial(_ffn_body, with_residual=with_residual),
        grid_spec=pltpu.PrefetchScalarGridSpec(
            num_scalar_prefetch=2,
            grid=(p // rows, nf),
            in_specs=in_specs,
            out_specs=pl.BlockSpec((rows, D_MODEL), lambda i, f, e, nv: (i, 0)),
            scratch_shapes=[pltpu.VMEM((rows, D_MODEL), F32)],
        ),
        out_shape=jax.ShapeDtypeStruct((p, D_MODEL), out_dtype),
        compiler_params=_params("parallel", "arbitrary"),
        name="ffn_residual" if with_residual else "ffn_grouped",
    )(blk_e, n_live, *args)


def _final_body(x_ref, y0_ref, y1_ref, gt_ref, g_ref, o_ref):
    gt = gt_ref[...]
    x = x_ref[...] + gt[:, 0:1] * y0_ref[...].astype(F32) + gt[:, 1:2] * y1_ref[...].astype(F32)
    ms = jnp.mean(x * x, axis=-1, keepdims=True)
    o_ref[...] = x * lax.rsqrt(ms + EPS) * g_ref[...]


def _final(x2, y0, y1, gates_pad, ln_g):
    n = x2.shape[0]
    tm = ROW_TILE
    row = lambda w: pl.BlockSpec((tm, w), lambda i: (i, 0))
    return pl.pallas_call(
        _final_body,
        grid=(n // tm,),
        in_specs=[row(D_MODEL), row(D_MODEL), row(D_MODEL), row(LANES), pl.BlockSpec((1, D_MODEL), lambda i: (0, 0))],
        out_specs=row(D_MODEL),
        out_shape=jax.ShapeDtypeStruct((n, D_MODEL), F32),
        compiler_params=_params("parallel"),
        name="moe_combine_final_norm",
    )(x2, y0, y1, gates_pad, ln_g.reshape(1, D_MODEL))


def _overlap_t(seq):
    nc = seq // CMP_STRIDE
    ns = seq // SLC_BLOCK
    cs = np.arange(nc)[None, :] * CMP_STRIDE
    ss = np.arange(ns)[:, None] * SLC_BLOCK
    return jnp.asarray(((cs < ss + SLC_BLOCK) & (cs + CMP_LEN > ss)).astype(np.float32), BF16)


def _route(logits, rows):
    n = logits.shape[0]
    top_logit, top_e = lax.top_k(logits, TOP_K)
    gates = jax.nn.softmax(top_logit, axis=-1)
    e_flat = top_e.reshape(-1).astype(jnp.int32)
    onehot = (e_flat[:, None] == jnp.arange(N_EXPERTS, dtype=jnp.int32)[None, :]).astype(jnp.int32)
    csum = jnp.cumsum(onehot, axis=0)
    counts = csum[-1]
    padded = ((counts + rows - 1) // rows) * rows
    starts = jnp.cumsum(counts) - counts
    pends = jnp.cumsum(padded)
    pstarts = pends - padded
    pos = jnp.sum(onehot * (csum - 1 + pstarts[None, :]), axis=1).reshape(n, TOP_K)
    p = n * TOP_K + N_EXPERTS * rows
    nblk = p // rows
    blk_e = jnp.minimum(jnp.searchsorted(pends, jnp.arange(nblk) * rows, side='right'), N_EXPERTS - 1)
    blk_e = blk_e.astype(jnp.int32)
    order = jnp.argsort(e_flat, stable=True).astype(jnp.int32)
    slot_e = jnp.repeat(blk_e, rows)
    off = jnp.arange(p, dtype=jnp.int32) - pstarts[slot_e]
    live = (off < counts[slot_e]) & (jnp.arange(p) < pends[-1])
    src = jnp.clip(starts[slot_e] + off, 0, n * TOP_K - 1)
    buf_tok = jnp.where(live, order[src] // TOP_K, 0).astype(jnp.int32)
    n_live = (pends[-1:] // rows).astype(jnp.int32)
    return gates, buf_tok, pos, blk_e, n_live


def _token_mixer(x2, l, batch, seq, tables, ovt, ln_attn, w_in, cmp_pe, cmp_w1, cmp_w2, conv_w, conv_b, conv_ln_g,
                 conv_ln_b, conv_pw, ret_gn):
    hm, ck, wd, gates, conv_in, ret = _norm_proj(x2, ln_attn[l], _arrange_w_in(w_in[l]), seq)
    nsub = seq // CMP_STRIDE
    half = CMP_STRIDE * HEAD_DIM
    sub = ck.reshape(CK_SLABS, batch, nsub, half)
    w1 = cmp_w1[l]
    w1cat = jnp.concatenate([w1[:, :half], w1[:, half:]], axis=2).astype(BF16)
    pe8 = jnp.broadcast_to(cmp_pe[l].reshape(2, 1, CMP_LEN * HEAD_DIM), (2, 8, CMP_LEN * HEAD_DIM)).astype(BF16)
    kvc = _compress(sub, pe8, w1cat, cmp_w2[l].astype(BF16))
    o_a = _nsa(hm, wd, gates, kvc, ovt, batch, seq)
    w_taps = jnp.pad(conv_w[l], ((0, CONV_TAP_ROWS - CONV_K), (0, 0)))
    o_b = _conv(conv_in, w_taps, conv_b[l], conv_ln_g[l], conv_ln_b[l], conv_pw[l].astype(BF16), batch, seq)
    o_c = _retention(ret, ret_gn[l], tables, batch, seq)
    return o_a, o_b, o_c


def kernel(x, ln_attn, w_in, cmp_pe, cmp_w1, cmp_w2, conv_w, conv_b, conv_ln_g, conv_ln_b, conv_pw, ret_gn, w_out,
           ln_ffn, ffn_w13, ffn_w2, router, moe_w13, moe_w2, ln_final):
    batch, seq, d = x.shape
    n = batch * seq
    depth = ln_attn.shape[0]
    x2 = x.reshape(n, d)
    tables = _retention_tables(seq)
    ovt = _overlap_t(seq)
    out = None
    for l in range(depth):
        o_a, o_b, o_c = _token_mixer(x2, l, batch, seq, tables, ovt, ln_attn, w_in, cmp_pe, cmp_w1, cmp_w2, conv_w,
                                     conv_b, conv_ln_g, conv_ln_b, conv_pw, ret_gn)
        if l % 2 == 0:
            x2, h2 = _out_proj(o_a, o_b, o_c, x2, w_out[l], ln_ffn[l])
            nblk = n // FFN_ROWS
            x2 = _ffn(h2, jnp.zeros((nblk,), jnp.int32), jnp.full((1,), nblk, jnp.int32),
                      ffn_w13[l // 2][None].astype(BF16), ffn_w2[l // 2][None].astype(BF16), FFN_ROWS, F32, residual=x2)
        else:
            x2, h2, logits = _out_proj(o_a, o_b, o_c, x2, w_out[l], ln_ffn[l], router=router[l // 2])
            gates, buf_tok, pos, blk_e, n_live = _route(logits[:, :N_EXPERTS], MOE_ROWS)
            xs = h2[buf_tok]
            yb = _ffn(xs, blk_e, n_live, moe_w13[l // 2].astype(BF16), moe_w2[l // 2].astype(BF16), MOE_ROWS, BF16)
            gates_pad = jnp.pad(gates, ((0, 0), (0, LANES - TOP_K)))
            if l == depth - 1:
                out = _final(x2, yb[pos[:, 0]], yb[pos[:, 1]], gates_pad, ln_final)
            else:
                x2 = x2 + gates[:, 0:1] * yb[pos[:, 0]] + gates[:, 1:2] * yb[pos[:, 1]]
    if out is None:
        out = _final(x2, jnp.zeros_like(x2), jnp.zeros_like(x2), jnp.zeros((n, LANES), F32), ln_final)
    return out.reshape(batch, seq, d)
```

```python
import functools

import numpy as np
import jax
import jax.numpy as jnp
from jax import lax
from jax.experimental import pallas as pl
from jax.experimental.pallas import tpu as pltpu

D_MODEL = 1024
HEAD_DIM = 64
NSA_HEADS = 8
NSA_KV_HEADS = 2
NSA_GROUP = NSA_HEADS // NSA_KV_HEADS
CMP_LEN = 32
CMP_STRIDE = 16
CMP_HIDDEN = 256
SLC_BLOCK = 64
SLC_TOPK = 16
WINDOW = 512
N_BRANCH = 3
CONV_CH = 256
CONV_K = 31
RET_HEADS = 4
RET_CHUNK = 128
ROPE_BASE = 10000.0
D_FF = 2816
N_EXPERTS = 8
TOP_K = 2
NEG_INF = -1e30
FORCE_SCORE = 1e9
EPS = 1e-6

NSA_Q_W = NSA_HEADS * HEAD_DIM
NSA_KV_W = NSA_KV_HEADS * HEAD_DIM
NSA_GATE_W = NSA_HEADS * N_BRANCH
RET_W = RET_HEADS * HEAD_DIM

LANES = 128
SUBLANES = 8
Q_TILE = 512
WIN_TILE = 256
ROW_BLK = 32
SEL_CHUNK = 512
ROW_TILE = 512
RET_BATCH = 8
CONV_TILE = 512
CONV_HALO = 32
CONV_TAP_ROWS = 32
MOE_ROWS = 512
FFN_ROWS = 512
FF_TILE = 1408
V7X_VMEM_BYTES = 64 * 1024 * 1024
VMEM_LIMIT = V7X_VMEM_BYTES * 7 // 8

BF16 = jnp.bfloat16
F32 = jnp.float32

HM_Q, HM_KW, HM_SLABS = 0, 8, 10
CK_SLABS = 4
WD_KS, WD_VS, WD_VW, WD_SLABS = 0, 2, 4, 6
LOG2E = 1.4426950408889634


def _dot(a, b):
    return jnp.dot(a, b, preferred_element_type=F32)


def _dot_nt(a, b):
    return lax.dot_general(a, b, (((1,), (1,)), ((), ())), preferred_element_type=F32)


def _split3(x):
    hi = x.astype(BF16)
    r1 = x - hi.astype(F32)
    mid = r1.astype(BF16)
    lo = (r1 - mid.astype(F32)).astype(BF16)
    return hi, mid, lo


def _params(*sem):
    return pltpu.CompilerParams(dimension_semantics=sem, vmem_limit_bytes=VMEM_LIMIT)


W_HM = (HM_SLABS + CK_SLABS) * HEAD_DIM
W_WD = W_HM + 3 * NSA_KV_W
W_GATE = W_WD + 2 * LANES
W_CONV = W_GATE + 2 * CONV_CH
W_RET = W_CONV + 4 * RET_W


def _arrange_w_in(w_in):
    offs = np.cumsum([0, NSA_Q_W] + [NSA_KV_W] * 6 + [NSA_GATE_W, 2 * CONV_CH] + [RET_W] * 4)
    q, kc, vc, ks, vs, kw, vw, gl, conv, rq, rk, rv, rg = [w_in[:, offs[i]:offs[i + 1]] for i in range(13)]
    scale = HEAD_DIM ** -0.5
    gpad = jnp.zeros((D_MODEL, LANES - NSA_GATE_W // 2), w_in.dtype)
    gates = jnp.concatenate([gl[:, :NSA_GATE_W // 2], gpad, gl[:, NSA_GATE_W // 2:], gpad], axis=1)
    w = jnp.concatenate([q * (scale * LOG2E), kw, kc, vc, ks, vs, vw, gates, conv, rq, rk * scale, rv, rg], axis=1)
    return w.astype(BF16)


def _norm_proj_body(x_ref, g_ref, w_ref, hm_ref, ck_ref, wd_ref, gates_ref, conv_ref, ret_ref, ck_scr, *, seq):
    x = x_ref[...]
    ms = jnp.mean(x * x, axis=-1, keepdims=True)
    h = (x * lax.rsqrt(ms + EPS) * g_ref[...]).astype(BF16)
    res = _dot(h, w_ref[:, 0:W_HM])
    for j in range(HM_SLABS):
        hm_ref[j] = res[:, j * HEAD_DIM:(j + 1) * HEAD_DIM].astype(BF16)
    nrow = x.shape[0] // CMP_STRIDE
    for k in range(ck_scr.shape[0]):
        c0 = HM_SLABS * HEAD_DIM + k * LANES
        ck_scr[k] = res[:, c0:c0 + LANES]
        toks = [ck_scr[k, pl.ds(t, nrow, stride=CMP_STRIDE), :] for t in range(CMP_STRIDE)]
        for half in range(LANES // HEAD_DIM):
            cols = slice(half * HEAD_DIM, (half + 1) * HEAD_DIM)
            ck_ref[k * (LANES // HEAD_DIM) + half] = jnp.concatenate([tk[:, cols] for tk in toks], axis=1).astype(BF16)
    rows = x.shape[0]
    t0 = (pl.program_id(0) % (seq // rows)) * rows
    lane = lax.broadcasted_iota(jnp.int32, (rows, HEAD_DIM), 1)
    blk = (t0 + lax.broadcasted_iota(jnp.int32, (rows, HEAD_DIM), 0)) // SLC_BLOCK
    onehot = jnp.where(blk == lane, 1.0, 0.0)
    ones_col = jnp.where(lane == 0, 1.0, 0.0)
    wide = _dot(h, w_ref[:, W_HM:W_WD])
    for j, ext in ((WD_KS, onehot), (WD_VS, ones_col), (WD_VW, ones_col)):
        for g in range(NSA_KV_HEADS):
            c = (j + g) * HEAD_DIM
            wd_ref[j + g] = jnp.concatenate([wide[:, c:c + HEAD_DIM], ext], axis=1).astype(BF16)
    gates_ref[...] = _dot(h, w_ref[:, W_WD:W_GATE])
    conv_ref[...] = _dot(h, w_ref[:, W_GATE:W_CONV]).astype(BF16)
    ret_ref[...] = _dot(h, w_ref[:, W_CONV:W_RET]).astype(BF16)


def _norm_proj(x2, ln_g, w_arr, seq):
    n = x2.shape[0]
    tm = ROW_TILE
    return pl.pallas_call(
        functools.partial(_norm_proj_body, seq=seq),
        grid=(n // tm,),
        in_specs=[
            pl.BlockSpec((tm, D_MODEL), lambda i: (i, 0)),
            pl.BlockSpec((1, D_MODEL), lambda i: (0, 0)),
            pl.BlockSpec((D_MODEL, W_RET), lambda i: (0, 0)),
        ],
        out_specs=[
            pl.BlockSpec((HM_SLABS, tm, HEAD_DIM), lambda i: (0, i, 0)),
            pl.BlockSpec((CK_SLABS, tm // CMP_STRIDE, CMP_STRIDE * HEAD_DIM), lambda i: (0, i, 0)),
            pl.BlockSpec((WD_SLABS, tm, LANES), lambda i: (0, i, 0)),
            pl.BlockSpec((tm, 2 * LANES), lambda i: (i, 0)),
            pl.BlockSpec((tm, 2 * CONV_CH), lambda i: (i, 0)),
            pl.BlockSpec((tm, 4 * RET_W), lambda i: (i, 0)),
        ],
        out_shape=[
            jax.ShapeDtypeStruct((HM_SLABS, n, HEAD_DIM), BF16),
            jax.ShapeDtypeStruct((CK_SLABS, n // CMP_STRIDE, CMP_STRIDE * HEAD_DIM), BF16),
            jax.ShapeDtypeStruct((WD_SLABS, n, LANES), BF16),
            jax.ShapeDtypeStruct((n, 2 * LANES), F32),
            jax.ShapeDtypeStruct((n, 2 * CONV_CH), BF16),
            jax.ShapeDtypeStruct((n, 4 * RET_W), BF16),
        ],
        scratch_shapes=[pltpu.VMEM((CK_SLABS * HEAD_DIM // LANES, tm, LANES), F32)],
        compiler_params=_params("parallel"),
        name="norm_proj",
    )(x2, ln_g.reshape(1, D_MODEL), w_arr)


def _compress_body(s_ref, pe_ref, w1_ref, w2_ref, o_ref):
    u = _dot(s_ref[...], w1_ref[...])
    pe = pe_ref[...]
    bias = (_dot(pe[:, :CMP_STRIDE * HEAD_DIM], w1_ref[:, :CMP_HIDDEN])
            + _dot(pe[:, CMP_STRIDE * HEAD_DIM:], w1_ref[:, CMP_HIDDEN:]))[0:1]
    bot = u[:, CMP_HIDDEN:]
    bot_next = jnp.concatenate([bot[1:], bot[:1]], axis=0)
    hid = u[:, :CMP_HIDDEN] + bot_next + bias
    hid = hid * jax.nn.sigmoid(hid)
    o_ref[...] = _dot(hid.astype(BF16), w2_ref[...]).astype(BF16)


def _compress(sub, pe8, w1cat, w2):
    _, b, nsub, width = sub.shape
    return pl.pallas_call(
        _compress_body,
        grid=(4, b),
        in_specs=[
            pl.BlockSpec((None, None, nsub, width), lambda j, i: (j, i, 0, 0)),
            pl.BlockSpec((None, 8, 2 * width), lambda j, i: (j // 2, 0, 0)),
            pl.BlockSpec((None, width, 2 * CMP_HIDDEN), lambda j, i: (j // 2, 0, 0)),
            pl.BlockSpec((None, CMP_HIDDEN, HEAD_DIM), lambda j, i: (j // 2, 0, 0)),
        ],
        out_specs=pl.BlockSpec((None, None, nsub, HEAD_DIM), lambda j, i: (j, i, 0, 0)),
        out_shape=jax.ShapeDtypeStruct((4, b, nsub, HEAD_DIM), BF16),
        compiler_params=_params("parallel", "parallel"),
        name="compress",
    )(sub, pe8, w1cat, w2)


def _topk_block_bias(imp_t):
    ns, nq = imp_t.shape
    sub = SUBLANES
    nb = ns // sub
    blocks = [imp_t[v * sub:(v + 1) * sub] for v in range(nb)]
    cnts = [jnp.zeros((sub, nq), jnp.int32) for _ in range(nb)]
    sub_iota = lax.broadcasted_iota(jnp.int32, (sub, nq), 0)
    for i in range(ns):
        row = jnp.broadcast_to(imp_t[i:i + 1], (sub, nq))
        for v in range(nb):
            if v > i // sub:
                cnts[v] = cnts[v] + jnp.where(row >= blocks[v], 1, 0)
            elif v < i // sub:
                cnts[v] = cnts[v] + jnp.where(row > blocks[v], 1, 0)
            else:
                tie = (row == blocks[v]) & (sub_iota > i % sub)
                cnts[v] = cnts[v] + jnp.where(row > blocks[v], 1, 0) + jnp.where(tie, 1, 0)
    cnt = jnp.concatenate(cnts, axis=0)
    return jnp.where(cnt < SLC_TOPK, 0.0, NEG_INF)


def _flash_update(qs, k, v1, bias, state):
    nh = len(state)
    qn = qs.shape[0] // nh
    s_all = _dot_nt(qs, k)
    ps, ms, alphas = [], [], []
    for h in range(nh):
        s = s_all[h * qn:(h + 1) * qn]
        if bias is not None:
            s = s + bias
        m = state[h][0]
        m_new = jnp.maximum(m, jnp.max(s, axis=-1, keepdims=True))
        ps.append(jnp.exp2(s - m_new).astype(BF16))
        alphas.append(jnp.exp2(m - m_new))
        ms.append(m_new)
    pv = _dot(jnp.concatenate(ps, axis=0), v1)
    return tuple((ms[h], alphas[h] * state[h][1] + pv[h * qn:(h + 1) * qn]) for h in range(nh))


def _flash_update_staged(qs, k, v1, bias, nh, s_scr, p_scr, m_scr, al_scr, acc_scr):
    rows, width = qs.shape[0], k.shape[0]
    qn = rows // nh
    s_scr[...] = _dot_nt(qs, k)
    for b in range(rows // ROW_BLK):
        r = slice(b * ROW_BLK, (b + 1) * ROW_BLK)
        q0 = (b * ROW_BLK) % qn
        s = s_scr[r] + bias[q0:q0 + ROW_BLK]
        m_old = m_scr[r]
        m_new = jnp.maximum(m_old, jnp.broadcast_to(jnp.max(s, axis=-1, keepdims=True), (ROW_BLK, LANES)))
        al_scr[r] = jnp.exp2(m_old - m_new)
        m_scr[r] = m_new
        p_scr[r] = jnp.exp2(s - jnp.concatenate([m_new] * (width // LANES), axis=1)).astype(BF16)
    acc_scr[...] = al_scr[...] * acc_scr[...] + _dot(p_scr[...], v1)


def _nsa_body(q_ref, gl_ref, kc_ref, vc_ref, ks_ref, vs_ref, kw_ref, vw_ref, ovt_ref, o_ref,
              s_scr, p_scr, m_sel, al_sel, acc_sel):
    i = pl.program_id(2)
    hpg = NSA_GROUP
    q = q_ref[...].reshape(hpg * Q_TILE, HEAD_DIM)
    rr = lax.broadcasted_iota(jnp.int32, (Q_TILE, 1), 0)
    t_q = i * Q_TILE + rr
    heads = [slice(h * Q_TILE, (h + 1) * Q_TILE) for h in range(hpg)]

    kc = kc_ref[...]
    ncb = kc.shape[0]
    s_c = _dot_nt(q, kc)
    c_end = lax.broadcasted_iota(jnp.int32, (1, ncb), 1) * CMP_STRIDE + (CMP_LEN - 1)
    cbias = jnp.where(c_end <= t_q, 0.0, NEG_INF)
    some_block = t_q >= CMP_LEN - 1
    pcs, psum = [], None
    for r in heads:
        s = s_c[r] + cbias
        p = jnp.exp2(s - jnp.max(s, axis=-1, keepdims=True))
        l = jnp.sum(p, axis=-1, keepdims=True)
        p = p * jnp.where(some_block, 1.0 / l, 0.0)
        pcs.append(p.astype(BF16))
        psum = p if psum is None else psum + p
    o_cmp = _dot(jnp.concatenate(pcs, axis=0), vc_ref[...])

    ovt = ovt_ref[...]
    imp_t = sum(_dot_nt(ovt, part) for part in _split3(psum))
    ns = imp_t.shape[0]
    sblk = lax.broadcasted_iota(jnp.int32, (ns, Q_TILE), 0)
    cur = (i * Q_TILE + lax.broadcasted_iota(jnp.int32, (ns, Q_TILE), 1)) // SLC_BLOCK
    forced = (sblk == 0) | (sblk == cur) | (sblk == cur - 1)
    imp_t = jnp.where(sblk <= cur, jnp.where(forced, FORCE_SCORE, imp_t), NEG_INF)
    few_blocks = (i + 1) * Q_TILE <= SLC_TOPK * SLC_BLOCK
    bias_t = lax.cond(few_blocks, lambda x: jnp.where(x > 0.5 * NEG_INF, 0.0, NEG_INF), _topk_block_bias, imp_t)
    bias_pad = jnp.concatenate([bias_t, jnp.zeros((LANES - ns, Q_TILE), F32)], axis=0).T
    bias = bias_pad[:, :ns].astype(BF16)
    qp = jnp.concatenate([q, jnp.concatenate([bias] * hpg, axis=0)], axis=1)

    m_sel[...] = jnp.full(m_sel.shape, NEG_INF, F32)
    acc_sel[...] = jnp.zeros(acc_sel.shape, F32)
    n_full = (i * Q_TILE) // SEL_CHUNK

    def sel_step(c, carry):
        k0 = pl.multiple_of(c * SEL_CHUNK, SEL_CHUNK)
        kpos = k0 + lax.broadcasted_iota(jnp.int32, (1, SEL_CHUNK), 1)
        causal = jnp.where(kpos <= t_q, 0.0, NEG_INF)
        _flash_update_staged(qp, ks_ref[pl.ds(k0, SEL_CHUNK), :], vs_ref[pl.ds(k0, SEL_CHUNK), :], causal, hpg,
                             s_scr, p_scr, m_sel, al_sel, acc_sel)
        return carry

    lax.fori_loop(0, n_full + 1, sel_step, 0)

    wkeys = WINDOW + WIN_TILE
    init_w = tuple((jnp.full((WIN_TILE, 1), NEG_INF, F32), jnp.zeros((WIN_TILE, LANES), F32)) for _ in range(hpg))
    st_w = []
    for u in range(Q_TILE // WIN_TILE):
        sub = slice(u * WIN_TILE, (u + 1) * WIN_TILE)
        q_u = jnp.concatenate([q[r][sub] for r in heads], axis=0)
        w0 = pl.multiple_of(jnp.maximum(i * Q_TILE + u * WIN_TILE - WINDOW, 0), WIN_TILE)
        dpos = t_q[sub] - (w0 + lax.broadcasted_iota(jnp.int32, (1, wkeys), 1))
        band = jnp.where((dpos >= 0) & (dpos < WINDOW), 0.0, NEG_INF)
        st_w.append(_flash_update(q_u, kw_ref[pl.ds(w0, wkeys), :], vw_ref[pl.ds(w0, wkeys), :], band, init_w))

    sig = jax.nn.sigmoid(gl_ref[...])
    outs = []
    for h, r in enumerate(heads):
        c = h * N_BRANCH
        acc_s = acc_sel[r]
        acc_w = jnp.concatenate([st_u[h][1] for st_u in st_w], axis=0)
        g_s = sig[:, c + 1:c + 2] / acc_s[:, HEAD_DIM:HEAD_DIM + 1]
        g_w = sig[:, c + 2:c + 3] / acc_w[:, HEAD_DIM:HEAD_DIM + 1]
        outs.append(sig[:, c:c + 1] * o_cmp[r] + g_s * acc_s[:, :HEAD_DIM] + g_w * acc_w[:, :HEAD_DIM])
    o_ref[...] = jnp.concatenate(outs, axis=1).astype(BF16)


def _nsa(hm, wd, gates, kvc, ovt, batch, seq):
    n = batch * seq
    nq = seq // Q_TILE
    ncb = kvc.shape[2]
    g_ = NSA_KV_HEADS
    rows = NSA_GROUP * Q_TILE
    return pl.pallas_call(
        _nsa_body,
        grid=(batch, g_, nq),
        in_specs=[
            pl.BlockSpec((NSA_GROUP, Q_TILE, HEAD_DIM), lambda b, g, i: (g, b * nq + i, 0)),
            pl.BlockSpec((Q_TILE, LANES), lambda b, g, i: (b * nq + i, g)),
            pl.BlockSpec((None, None, ncb, HEAD_DIM), lambda b, g, i: (g, b, 0, 0)),
            pl.BlockSpec((None, None, ncb, HEAD_DIM), lambda b, g, i: (g_ + g, b, 0, 0)),
            pl.BlockSpec((None, seq, LANES), lambda b, g, i: (WD_KS + g, b, 0)),
            pl.BlockSpec((None, seq, LANES), lambda b, g, i: (WD_VS + g, b, 0)),
            pl.BlockSpec((None, seq, HEAD_DIM), lambda b, g, i: (HM_KW + g, b, 0)),
            pl.BlockSpec((None, seq, LANES), lambda b, g, i: (WD_VW + g, b, 0)),
            pl.BlockSpec(ovt.shape, lambda b, g, i: (0, 0)),
        ],
        out_specs=pl.BlockSpec((Q_TILE, NSA_GROUP * HEAD_DIM), lambda b, g, i: (b * nq + i, g)),
        out_shape=jax.ShapeDtypeStruct((n, NSA_Q_W), BF16),
        scratch_shapes=[
            pltpu.VMEM((rows, SEL_CHUNK), F32), pltpu.VMEM((rows, SEL_CHUNK), BF16),
            pltpu.VMEM((rows, LANES), F32), pltpu.VMEM((rows, LANES), F32), pltpu.VMEM((rows, LANES), F32),
        ],
        compiler_params=_params("parallel", "parallel", "arbitrary"),
        name="nsa",
    )(hm, gates, kvc, kvc, wd, wd, hm, wd, ovt)


def _conv_body(u_ref, up_ref, w_ref, b_ref, lg_ref, lb_ref, pw_ref, o_ref, hbuf, sbuf):
    i = pl.program_id(1)

    def glu(u):
        u = u.astype(F32)
        return u[:, :CONV_CH] * jax.nn.sigmoid(u[:, CONV_CH:])

    prev = glu(up_ref[...])
    hbuf[0:CONV_HALO] = jnp.where(i > 0, prev, 0.0)
    hbuf[CONV_HALO:] = glu(u_ref[...])
    span = sbuf.shape[1]
    for s in range(1, SUBLANES):
        sbuf[s - 1] = hbuf[s:s + span]
    w = w_ref[...]
    acc = jnp.zeros((CONV_TILE, CONV_CH), F32) + b_ref[...]
    base = CONV_HALO - (CONV_K - 1)
    for j in range(CONV_K):
        s = (base + j) % SUBLANES
        a8 = base + j - s
        src = hbuf[a8:a8 + CONV_TILE] if s == 0 else sbuf[s - 1, a8:a8 + CONV_TILE]
        acc = acc + w[j:j + 1] * src
    mu = jnp.mean(acc, axis=-1, keepdims=True)
    var = jnp.mean(jnp.square(acc - mu), axis=-1, keepdims=True)
    hf = (acc - mu) * lax.rsqrt(var + 1e-5) * lg_ref[...] + lb_ref[...]
    hf = hf * jax.nn.sigmoid(hf)
    o_ref[...] = _dot(hf.astype(BF16), pw_ref[...]).astype(BF16)


def _conv(conv_in, w_taps, b, lg, lb, pw, batch, seq):
    n = batch * seq
    nt = seq // CONV_TILE
    halo_per_tile = CONV_TILE // CONV_HALO
    row = lambda v: v.reshape(1, CONV_CH)
    return pl.pallas_call(
        _conv_body,
        grid=(batch, nt),
        in_specs=[
            pl.BlockSpec((CONV_TILE, 2 * CONV_CH), lambda b_, i: (b_ * nt + i, 0)),
            pl.BlockSpec((CONV_HALO, 2 * CONV_CH),
                         lambda b_, i: (jnp.maximum((b_ * nt + i) * halo_per_tile - 1, 0), 0)),
            pl.BlockSpec((CONV_TAP_ROWS, CONV_CH), lambda b_, i: (0, 0)),
            pl.BlockSpec((1, CONV_CH), lambda b_, i: (0, 0)),
            pl.BlockSpec((1, CONV_CH), lambda b_, i: (0, 0)),
            pl.BlockSpec((1, CONV_CH), lambda b_, i: (0, 0)),
            pl.BlockSpec((CONV_CH, CONV_CH), lambda b_, i: (0, 0)),
        ],
        out_specs=pl.BlockSpec((CONV_TILE, CONV_CH), lambda b_, i: (b_ * nt + i, 0)),
        out_shape=jax.ShapeDtypeStruct((n, CONV_CH), BF16),
        scratch_shapes=[pltpu.VMEM((CONV_HALO + CONV_TILE, CONV_CH), F32),
                        pltpu.VMEM((SUBLANES - 1, CONV_HALO + CONV_TILE - SUBLANES, CONV_CH), F32)],
        compiler_params=_params("parallel", "parallel"),
        name="conformer_conv",
    )(conv_in, conv_in, w_taps, row(b), row(lg), row(lb), pw)


def _ret_body(q_ref, k_ref, v_ref, g_ref, cos_ref, sin_ref, perm_ref, decay_ref, xi_ref, zeta_ref, maskk_ref,
              maskv_ref, gmat_ref, masks_ref, mavg_ref, gn_ref, o_ref, state):
    c = pl.program_id(1)

    @pl.when(c == 0)
    def _():
        state[...] = jnp.zeros_like(state)

    nb = q_ref.shape[0]
    chunk = q_ref.shape[1]
    cos, sin = cos_ref[...], sin_ref[...]
    qk = jnp.concatenate([q_ref[bi] for bi in range(nb)] + [k_ref[bi] for bi in range(nb)], axis=0)
    rot = _dot(qk, perm_ref[...])
    outs, new_states = [], []
    for bi in range(nb):
        qs, ks = slice(bi * chunk, (bi + 1) * chunk), slice((nb + bi) * chunk, (nb + bi + 1) * chunk)
        qb = (q_ref[bi].astype(F32) * cos + rot[qs] * sin).astype(BF16)
        kr_t = (k_ref[bi].astype(F32) * cos + rot[ks] * sin).T
        k_bd = jnp.concatenate([kr_t.astype(BF16)] * RET_HEADS, axis=1) * maskk_ref[...]
        a = _dot(qb, k_bd) * decay_ref[...]
        v = v_ref[bi]
        v_bd = jnp.concatenate([v] * RET_HEADS, axis=0) * maskv_ref[...]
        s_bd = state[bi]
        outs.append(_dot(a.astype(BF16), v_bd) + _dot(qb, s_bd.astype(BF16)) * xi_ref[...])
        kz_t = (kr_t * zeta_ref[...]).astype(BF16)
        new_states.append(gmat_ref[...] * s_bd + _dot(kz_t, v) * masks_ref[...])
    o = jnp.concatenate(outs, axis=0)

    def head_mean(x):
        hi = x.astype(BF16)
        lo = (x - hi.astype(F32)).astype(BF16)
        return _dot(hi, mavg_ref[...]) + _dot(lo, mavg_ref[...])

    d = o - head_mean(o)
    y = d * lax.rsqrt(head_mean(d * d) + 1e-5) * gn_ref[...]
    for bi in range(nb):
        gate = g_ref[bi].astype(F32)
        o_ref[bi] = (gate * jax.nn.sigmoid(gate) * y[bi * chunk:(bi + 1) * chunk]).astype(BF16)
        state[bi] = new_states[bi]


def _retention_tables(seq):
    half = HEAD_DIM // 2
    pos = jnp.arange(seq, dtype=F32)
    inv = ROPE_BASE ** (-jnp.arange(half, dtype=F32) / half)
    ang = pos[:, None] * inv[None, :]
    cos, sin = jnp.cos(ang), jnp.sin(ang)
    cos_t = jnp.tile(jnp.concatenate([cos, cos], axis=1), (1, RET_HEADS))
    sin_t = jnp.tile(jnp.concatenate([-sin, sin], axis=1), (1, RET_HEADS))
    log_gamma = jnp.log1p(-jnp.exp2(-5.0 - jnp.arange(RET_HEADS, dtype=F32)))
    i = jnp.arange(RET_CHUNK, dtype=F32)
    rel = i[:, None] - i[None, :]
    decay = jnp.where(rel >= 0, jnp.exp(jnp.maximum(rel, 0.0)[None] * log_gamma[:, None, None]), 0.0)
    decay_cat = decay.transpose(1, 0, 2).reshape(RET_CHUNK, RET_HEADS * RET_CHUNK)
    xi = jnp.exp((i + 1.0)[None, :] * log_gamma[:, None])
    zeta = jnp.exp((RET_CHUNK - 1.0 - i)[None, :] * log_gamma[:, None])
    g_chunk = jnp.exp(RET_CHUNK * log_gamma)
    lanes = lambda a: jnp.repeat(a.T, HEAD_DIM, axis=1)
    j = np.arange(RET_W)
    partner = (j // HEAD_DIM) * HEAD_DIM + (j % HEAD_DIM + half) % HEAD_DIM
    perm = np.zeros((RET_W, RET_W), np.float32)
    perm[partner, j] = 1.0
    head_of_feat = j // HEAD_DIM
    head_of_key = np.arange(RET_HEADS * RET_CHUNK) // RET_CHUNK
    maskk = (head_of_feat[:, None] == head_of_key[None, :]).astype(np.float32)
    masks = (head_of_feat[:, None] == head_of_feat[None, :]).astype(np.float32)
    gmat = jnp.asarray(masks) * jnp.repeat(g_chunk, HEAD_DIM)[:, None]
    return (cos_t, sin_t, jnp.asarray(perm, BF16), decay_cat, lanes(xi), lanes(zeta).T, jnp.asarray(maskk, BF16),
            jnp.asarray(maskk.T, BF16), gmat, jnp.asarray(masks), jnp.asarray(masks / HEAD_DIM, BF16))


def _retention(ret, gn_g, tables, batch, seq):
    nc = seq // RET_CHUNK
    rb = RET_BATCH
    cos_t, sin_t = tables[:2]
    consts = tables[2:]
    ret3 = ret.reshape(batch, seq, 4 * RET_W)
    tile = lambda col: pl.BlockSpec((rb, RET_CHUNK, RET_W), lambda b, c: (b, c, col))
    const = lambda a: pl.BlockSpec(a.shape, lambda b, c: (0,) * a.ndim)
    out = pl.pallas_call(
        _ret_body,
        grid=(batch // rb, nc),
        in_specs=[
            tile(0), tile(1), tile(2), tile(3),
            pl.BlockSpec((RET_CHUNK, RET_W), lambda b, c: (c, 0)),
            pl.BlockSpec((RET_CHUNK, RET_W), lambda b, c: (c, 0)),
            *[const(a) for a in consts],
            pl.BlockSpec((1, RET_W), lambda b, c: (0, 0)),
        ],
        out_specs=pl.BlockSpec((rb, RET_CHUNK, RET_W), lambda b, c: (b, c, 0)),
        out_shape=jax.ShapeDtypeStruct((batch, seq, RET_W), BF16),
        scratch_shapes=[pltpu.VMEM((rb, RET_W, RET_W), F32)],
        compiler_params=_params("parallel", "arbitrary"),
        name="retention",
    )(ret3, ret3, ret3, ret3, cos_t, sin_t, *consts, gn_g.reshape(1, RET_W))
    return out.reshape(batch * seq, RET_W)


def _out_proj_body(*refs, with_router):
    if with_router:
        oa_ref, ob_ref, oc_ref, x_ref, w_ref, g_ref, rcat_ref, xo_ref, h_ref, lg_ref = refs
    else:
        oa_ref, ob_ref, oc_ref, x_ref, w_ref, g_ref, xo_ref, h_ref = refs
    y = (_dot(oa_ref[...], w_ref[0:NSA_Q_W])
         + _dot(ob_ref[...], w_ref[NSA_Q_W:NSA_Q_W + CONV_CH])
         + _dot(oc_ref[...], w_ref[NSA_Q_W + CONV_CH:]))
    x = x_ref[...] + y
    xo_ref[...] = x
    ms = jnp.mean(x * x, axis=-1, keepdims=True)
    h = x * lax.rsqrt(ms + EPS) * g_ref[...]
    hi = h.astype(BF16)
    h_ref[...] = hi
    if with_router:
        lo = (h - hi.astype(F32)).astype(BF16)
        t = _dot(hi, rcat_ref[...])
        logits = t[:, :LANES] + t[:, LANES:] + _dot(lo, rcat_ref[:, :LANES])
        lane = lax.broadcasted_iota(jnp.int32, logits.shape, 1)
        lg1 = jnp.where(lane < N_EXPERTS, logits, NEG_INF)
        m1 = jnp.max(lg1, axis=-1, keepdims=True)
        lane_f = lane.astype(F32)
        e1 = jnp.min(jnp.where(lg1 == m1, lane_f, float(LANES)), axis=-1, keepdims=True)
        lg2 = jnp.where(lane_f == e1, NEG_INF, lg1)
        m2 = jnp.max(lg2, axis=-1, keepdims=True)
        e2 = jnp.min(jnp.where(lg2 == m2, lane_f, float(LANES)), axis=-1, keepdims=True)
        ex = jnp.exp(m2 - m1)
        g1 = 1.0 / (1.0 + ex)
        cols = (e1, e2, g1, ex * g1)
        out = jnp.zeros(logits.shape, F32)
        for k, v in enumerate(cols):
            out = jnp.where(lane == k, v, out)
        lg_ref[...] = out


def _out_proj(oa, ob, oc, x2, w_out, ln_g, router=None):
    n = x2.shape[0]
    tm = ROW_TILE
    with_router = router is not None
    row = lambda w: pl.BlockSpec((tm, w), lambda i: (i, 0))
    const = lambda shape: pl.BlockSpec(shape, lambda i: (0, 0))
    in_specs = [row(NSA_Q_W), row(CONV_CH), row(RET_W), row(D_MODEL), const((D_MODEL, D_MODEL)), const((1, D_MODEL))]
    args = [oa, ob, oc, x2, w_out.astype(BF16), ln_g.reshape(1, D_MODEL)]
    out_specs = [row(D_MODEL), row(D_MODEL)]
    out_shape = [jax.ShapeDtypeStruct((n, D_MODEL), F32), jax.ShapeDtypeStruct((n, D_MODEL), BF16)]
    if with_router:
        rpad = jnp.pad(router, ((0, 0), (0, LANES - N_EXPERTS)))
        rhi = rpad.astype(BF16)
        rlo = (rpad - rhi.astype(F32)).astype(BF16)
        in_specs += [const((D_MODEL, 2 * LANES))]
        args += [jnp.concatenate([rhi, rlo], axis=1)]
        out_specs.append(row(LANES))
        out_shape.append(jax.ShapeDtypeStruct((n, LANES), F32))
    return pl.pallas_call(
        functools.partial(_out_proj_body, with_router=with_router),
        grid=(n // tm,),
        in_specs=in_specs,
        out_specs=out_specs,
        out_shape=out_shape,
        compiler_params=_params("parallel"),
        name="out_proj_router" if with_router else "out_proj",
    )(*args)


def _ffn_body(*refs, with_residual):
    if with_residual:
        e_ref, nv_ref, x_ref, wa_ref, wb_ref, w2_ref, r_ref, o_ref, acc = refs
    else:
        e_ref, nv_ref, x_ref, wa_ref, wb_ref, w2_ref, o_ref, acc = refs
    del e_ref
    f = pl.program_id(1)
    live = pl.program_id(0) < nv_ref[0]

    @pl.when(f == 0)
    def _():
        acc[...] = jnp.zeros_like(acc)

    @pl.when(live)
    def _():
        x = x_ref[...]
        a = _dot(x, wa_ref[...])
        b = _dot(x, wb_ref[...])
        hmid = (a * jax.nn.sigmoid(a) * b).astype(BF16)
        acc[...] += _dot(hmid, w2_ref[...])

    @pl.when(f == pl.num_programs(1) - 1)
    def _():
        o_ref[...] = (acc[...] + r_ref[...] if with_residual else acc[...]).astype(o_ref.dtype)


def _ffn(xs, blk_e, n_live, w13, w2, rows, out_dtype, residual=None):
    p = xs.shape[0]
    nf = D_FF // FF_TILE
    with_residual = residual is not None
    last = lambda i, nv: jnp.minimum(i, nv[0] - 1)
    in_specs = [
        pl.BlockSpec((rows, D_MODEL), lambda i, f, e, nv: (last(i, nv), 0)),
        pl.BlockSpec((None, D_MODEL, FF_TILE), lambda i, f, e, nv: (e[last(i, nv)], 0, f)),
        pl.BlockSpec((None, D_MODEL, FF_TILE), lambda i, f, e, nv: (e[last(i, nv)], 0, nf + f)),
        pl.BlockSpec((None, FF_TILE, D_MODEL), lambda i, f, e, nv: (e[last(i, nv)], f, 0)),
    ]
    args = [xs, w13, w13, w2]
    if with_residual:
        in_specs.append(pl.BlockSpec((rows, D_MODEL), lambda i, f, e, nv: (i, 0)))
        args.append(residual)
    return pl.pallas_call(
        functools.partial(_ffn_body, with_residual=with_residual),
        grid_spec=pltpu.PrefetchScalarGridSpec(
            num_scalar_prefetch=2,
            grid=(p // rows, nf),
            in_specs=in_specs,
            out_specs=pl.BlockSpec((rows, D_MODEL), lambda i, f, e, nv: (i, 0)),
            scratch_shapes=[pltpu.VMEM((rows, D_MODEL), F32)],
        ),
        out_shape=jax.ShapeDtypeStruct((p, D_MODEL), out_dtype),
        compiler_params=_params("parallel", "arbitrary"),
        name="ffn_residual" if with_residual else "ffn_grouped",
    )(blk_e, n_live, *args)


def _final_body(x_ref, y0_ref, y1_ref, gt_ref, g_ref, o_ref):
    gt = gt_ref[...]
    x = x_ref[...] + gt[:, 0:1] * y0_ref[...].astype(F32) + gt[:, 1:2] * y1_ref[...].astype(F32)
    ms = jnp.mean(x * x, axis=-1, keepdims=True)
    o_ref[...] = x * lax.rsqrt(ms + EPS) * g_ref[...]


def _final(x2, y0, y1, gates_pad, ln_g):
    n = x2.shape[0]
    tm = ROW_TILE
    row = lambda w: pl.BlockSpec((tm, w), lambda i: (i, 0))
    return pl.pallas_call(
        _final_body,
        grid=(n // tm,),
        in_specs=[row(D_MODEL), row(D_MODEL), row(D_MODEL), row(LANES), pl.BlockSpec((1, D_MODEL), lambda i: (0, 0))],
        out_specs=row(D_MODEL),
        out_shape=jax.ShapeDtypeStruct((n, D_MODEL), F32),
        compiler_params=_params("parallel"),
        name="moe_combine_final_norm",
    )(x2, y0, y1, gates_pad, ln_g.reshape(1, D_MODEL))


def _overlap_t(seq):
    nc = seq // CMP_STRIDE
    ns = seq // SLC_BLOCK
    cs = np.arange(nc)[None, :] * CMP_STRIDE
    ss = np.arange(ns)[:, None] * SLC_BLOCK
    return jnp.asarray(((cs < ss + SLC_BLOCK) & (cs + CMP_LEN > ss)).astype(np.float32), BF16)


def _route(logits, rows):
    n = logits.shape[0]
    top_e = logits[:, :TOP_K].astype(jnp.int32)
    gates = logits[:, TOP_K:2 * TOP_K]
    e_flat = top_e.reshape(-1).astype(jnp.int32)
    onehot = (e_flat[:, None] == jnp.arange(N_EXPERTS, dtype=jnp.int32)[None, :]).astype(jnp.int32)
    csum = jnp.cumsum(onehot, axis=0)
    counts = csum[-1]
    padded = ((counts + rows - 1) // rows) * rows
    starts = jnp.cumsum(counts) - counts
    pends = jnp.cumsum(padded)
    pstarts = pends - padded
    pos = jnp.sum(onehot * (csum - 1 + pstarts[None, :]), axis=1).reshape(n, TOP_K)
    p = n * TOP_K + N_EXPERTS * rows
    nblk = p // rows
    blk_e = jnp.minimum(jnp.searchsorted(pends, jnp.arange(nblk) * rows, side='right'), N_EXPERTS - 1)
    blk_e = blk_e.astype(jnp.int32)
    order = jnp.argsort(e_flat, stable=True).astype(jnp.int32)
    slot_e = jnp.repeat(blk_e, rows)
    off = jnp.arange(p, dtype=jnp.int32) - pstarts[slot_e]
    live = (off < counts[slot_e]) & (jnp.arange(p) < pends[-1])
    src = jnp.clip(starts[slot_e] + off, 0, n * TOP_K - 1)
    buf_tok = jnp.where(live, order[src] // TOP_K, 0).astype(jnp.int32)
    n_live = (pends[-1:] // rows).astype(jnp.int32)
    return gates, buf_tok, pos, blk_e, n_live


def _token_mixer(x2, l, batch, seq, tables, ovt, ln_attn, w_in, cmp_pe, cmp_w1, cmp_w2, conv_w, conv_b, conv_ln_g,
                 conv_ln_b, conv_pw, ret_gn):
    hm, ck, wd, gates, conv_in, ret = _norm_proj(x2, ln_attn[l], _arrange_w_in(w_in[l]), seq)
    nsub = seq // CMP_STRIDE
    half = CMP_STRIDE * HEAD_DIM
    sub = ck.reshape(CK_SLABS, batch, nsub, half)
    w1 = cmp_w1[l]
    w1cat = jnp.concatenate([w1[:, :half], w1[:, half:]], axis=2).astype(BF16)
    pe8 = jnp.broadcast_to(cmp_pe[l].reshape(2, 1, CMP_LEN * HEAD_DIM), (2, 8, CMP_LEN * HEAD_DIM)).astype(BF16)
    kvc = _compress(sub, pe8, w1cat, cmp_w2[l].astype(BF16))
    o_a = _nsa(hm, wd, gates, kvc, ovt, batch, seq)
    w_taps = jnp.pad(conv_w[l], ((0, CONV_TAP_ROWS - CONV_K), (0, 0)))
    o_b = _conv(conv_in, w_taps, conv_b[l], conv_ln_g[l], conv_ln_b[l], conv_pw[l].astype(BF16), batch, seq)
    o_c = _retention(ret, ret_gn[l], tables, batch, seq)
    return o_a, o_b, o_c


def kernel(x, ln_attn, w_in, cmp_pe, cmp_w1, cmp_w2, conv_w, conv_b, conv_ln_g, conv_ln_b, conv_pw, ret_gn, w_out,
           ln_ffn, ffn_w13, ffn_w2, router, moe_w13, moe_w2, ln_final):
    batch, seq, d = x.shape
    n = batch * seq
    depth = ln_attn.shape[0]
    x2 = x.reshape(n, d)
    tables = _retention_tables(seq)
    ovt = _overlap_t(seq)
    out = None
    for l in range(depth):
        o_a, o_b, o_c = _token_mixer(x2, l, batch, seq, tables, ovt, ln_attn, w_in, cmp_pe, cmp_w1, cmp_w2, conv_w,
                                     conv_b, conv_ln_g, conv_ln_b, conv_pw, ret_gn)
        if l % 2 == 0:
            x2, h2 = _out_proj(o_a, o_b, o_c, x2, w_out[l], ln_ffn[l])
            nblk = n // FFN_ROWS
            x2 = _ffn(h2, jnp.zeros((nblk,), jnp.int32), jnp.full((1,), nblk, jnp.int32),
                      ffn_w13[l // 2][None].astype(BF16), ffn_w2[l // 2][None].astype(BF16), FFN_ROWS, F32, residual=x2)
        else:
            x2, h2, logits = _out_proj(o_a, o_b, o_c, x2, w_out[l], ln_ffn[l], router=router[l // 2])
            gates, buf_tok, pos, blk_e, n_live = _route(logits, MOE_ROWS)
            xs = h2[buf_tok]
            yb = _ffn(xs, blk_e, n_live, moe_w13[l // 2].astype(BF16), moe_w2[l // 2].astype(BF16), MOE_ROWS, BF16)
            gates_pad = jnp.pad(gates, ((0, 0), (0, LANES - TOP_K)))
            if l == depth - 1:
                out = _final(x2, yb[pos[:, 0]], yb[pos[:, 1]], gates_pad, ln_final)
            else:
                x2 = x2 + gates[:, 0:1] * yb[pos[:, 0]] + gates[:, 1:2] * yb[pos[:, 1]]
    if out is None:
        out = _final(x2, jnp.zeros_like(x2), jnp.zeros_like(x2), jnp.zeros((n, LANES), F32), ln_final)
    return out.reshape(batch, seq, d)
```
